```python
import math
import jax, jax.numpy as jnp
from jax import lax
import numpy as np

D_MODEL = 1024
BATCH = 2
SEQ = 8192
DEPTH = 1
DEC_BATCH = 32
DEC_SEQ = 4
PAST_LEN = 16384
PAGE_SIZE = 128

ATT_HEADS = 8
ATT_HEAD_DIM = 64
ATT_WIDTH = ATT_HEADS * ATT_HEAD_DIM
MOBA_BLOCK = 256
MOBA_TOPK = 3
ATT_Q_BLOCK = 64
HG_HEADS = 4
HG_DK = 128
HG_DV = 128
HG_KWIDTH = HG_HEADS * HG_DK
HG_WIDTH = HG_HEADS * HG_DV
HG_CHUNK = 64
MIX_WIDTH = ATT_WIDTH + HG_WIDTH
IN_SPLITS = (ATT_WIDTH, 2 * ATT_WIDTH, 3 * ATT_WIDTH, 3 * ATT_WIDTH + HG_KWIDTH,
             3 * ATT_WIDTH + 2 * HG_KWIDTH, 3 * ATT_WIDTH + 2 * HG_KWIDTH + HG_WIDTH)
IN_COLS = 3 * ATT_WIDTH + 2 * HG_KWIDTH + 2 * HG_WIDTH
N_EXPERTS = 64
EXPERT_TOPK = 6
N_EXPERT_GROUPS = 8
TOPK_GROUPS = 4
EXPERT_DIM = 256
SHARED_DIM = 256
ROUTED_SCALE = 2.5
MOE_TOKEN_BLOCK = 512
RMS_EPS = 1e-6
NEG_INF = -1e30

kernel_name = 'hymba_moba_hgrn2_moe_adaln_step'


def rms_normalize(x):
    xf = x.astype(jnp.float32)
    return xf * lax.rsqrt(jnp.mean(xf * xf, axis=-1, keepdims=True) + RMS_EPS)


def rmsnorm(x, g):
    return (rms_normalize(x) * g.astype(jnp.float32)).astype(x.dtype)


def modulate(h, shift, scale):
    return h * (1 + scale[:, None, :]) + shift[:, None, :]


def alibi_slopes(n_heads):
    return jnp.asarray([2.0 ** (-8.0 * (i + 1) / n_heads) for i in range(n_heads)], dtype=jnp.float32)


def to_blocks(k):
    b, length, h, dh = k.shape
    nb = -(-length // MOBA_BLOCK)
    k = jnp.pad(k, ((0, 0), (0, nb * MOBA_BLOCK - length), (0, 0), (0, 0)))
    return k.reshape(b, nb, MOBA_BLOCK, h, dh)


def paged_blocks(pool, page_table, new_rows):
    db, n_pages = page_table.shape
    past = pool[page_table].astype(new_rows.dtype)
    past = past.reshape(db, n_pages * pool.shape[1], pool.shape[2], pool.shape[3])
    return to_blocks(jnp.concatenate([past, new_rows], axis=1))


def moba_attention(q, kb, vb, q_pos):
    b, t, h, dh = q.shape
    nb = kb.shape[1]
    k_sel = min(MOBA_TOPK, nb)
    n_slots = k_sel + 1
    means = jnp.mean(kb.astype(jnp.float32), axis=2)
    slopes = alibi_slopes(h)
    scale = dh ** -0.5
    qc = math.gcd(t, ATT_Q_BLOCK)
    n_chunks = t // qc
    bi = jnp.arange(b, dtype=jnp.int32)[:, None, None, None]
    hi = jnp.arange(h, dtype=jnp.int32)[None, None, :, None]
    r = jnp.arange(MOBA_BLOCK, dtype=jnp.int32)
    slot = jnp.arange(n_slots, dtype=jnp.int32)[None, None, None, :, None]

    def one_chunk(args):
        qch, pos = args
        own = pos // MOBA_BLOCK
        qf = qch.astype(jnp.float32)
        gate = jnp.einsum('bqhd,bnhd->bqhn', qf, means)
        past = jnp.arange(nb, dtype=jnp.int32)[None, :] < own[:, None]
        gate = jnp.where(past[None, :, None, :], gate, NEG_INF)
        _, top = lax.top_k(gate, k_sel)
        own_idx = jnp.broadcast_to(own[None, :, None, None], (b, qc, h, 1)).astype(top.dtype)
        idx = jnp.concatenate([top, own_idx], axis=-1)
        kg = kb[bi, idx, :, hi].astype(jnp.float32)
        vg = vb[bi, idx, :, hi].astype(jnp.float32)
        logits = jnp.einsum('bqhd,bqhjrd->bqhjr', qf * scale, kg)
        key_pos = idx[..., None] * MOBA_BLOCK + r
        pos_b = pos[None, :, None, None, None]
        logits = logits - slopes[None, None, :, None, None] * (pos_b - key_pos).astype(jnp.float32)
        valid = jnp.where(slot == k_sel, key_pos <= pos_b, slot < own[None, :, None, None, None])
        logits = jnp.where(valid, logits, NEG_INF)
        p = jax.nn.softmax(logits.reshape(b, qc, h, -1), axis=-1).reshape(logits.shape)
        return jnp.einsum('bqhjr,bqhjrd->bqhd', p, vg)

    q_chunks = q.reshape(b, n_chunks, qc, h, dh).transpose(1, 0, 2, 3, 4)
    pos_chunks = q_pos.reshape(n_chunks, qc)
    out = lax.map(one_chunk, (q_chunks, pos_chunks))
    return out.transpose(1, 0, 2, 3, 4).reshape(b, t, h, dh).astype(q.dtype)


def hgrn2_recurrence(q, logf, k, v, s0, chunk):
    b, t, h, dk = q.shape
    dv = v.shape[-1]
    c = math.gcd(t, chunk)
    n = t // c

    def to_chunks(a):
        return a.reshape(b, n, c, h, a.shape[-1]).transpose(1, 0, 3, 2, 4).astype(jnp.float32)

    tri = jnp.tril(jnp.ones((c, c), dtype=bool))[None, None, :, :, None]

    def step(s, xs):
        qc, lf, kc, vc = xs
        cum = jnp.cumsum(lf, axis=2)
        o_inter = jnp.einsum('bhtk,bhkv->bhtv', qc * jnp.exp(cum), s)
        diff = jnp.where(tri, cum[:, :, :, None, :] - cum[:, :, None, :, :], NEG_INF)
        attn = jnp.sum(qc[:, :, :, None, :] * kc[:, :, None, :, :] * jnp.exp(diff), axis=-1)
        o = o_inter + jnp.einsum('bhts,bhsv->bhtv', attn, vc)
        last = cum[:, :, -1:, :]
        s_new = jnp.exp(last[:, :, 0, :])[..., None] * s + jnp.einsum('bhsk,bhsv->bhkv', kc * jnp.exp(last - cum), vc)
        return s_new, o

    s_final, o = lax.scan(step, s0.astype(jnp.float32), (to_chunks(q), to_chunks(logf), to_chunks(k), to_chunks(v)))
    return o.transpose(1, 0, 3, 2, 4).reshape(b, t, h, dv), s_final


def route_gates(h, r_w, r_b):
    t = h.shape[0]
    scores = jax.nn.sigmoid(h.astype(jnp.float32) @ r_w.astype(jnp.float32))
    biased = scores + r_b.astype(jnp.float32)
    grouped = biased.reshape(t, N_EXPERT_GROUPS, N_EXPERTS // N_EXPERT_GROUPS)
    group_score = jnp.sum(lax.top_k(grouped, 2)[0], axis=-1)
    _, top_g = lax.top_k(group_score, TOPK_GROUPS)
    group_keep = jnp.any(top_g[:, :, None] == jnp.arange(N_EXPERT_GROUPS)[None, None, :], axis=1)
    expert_keep = jnp.repeat(group_keep, N_EXPERTS // N_EXPERT_GROUPS, axis=1)
    _, top_e = lax.top_k(jnp.where(expert_keep, biased, NEG_INF), EXPERT_TOPK)
    w = jnp.take_along_axis(scores, top_e, axis=1)
    w = w / jnp.sum(w, axis=1, keepdims=True) * ROUTED_SCALE
    return jnp.sum(jax.nn.one_hot(top_e, N_EXPERTS, dtype=jnp.float32) * w[..., None], axis=1)


def moe_ffn(h, r_w, r_b, e_w1, e_w3, e_w2, sh_w1, sh_w3, sh_w2):
    t, d = h.shape
    gates = route_gates(h, r_w, r_b).astype(h.dtype)
    blk = min(MOE_TOKEN_BLOCK, t)
    n = -(-t // blk)
    pad = n * blk - t
    hp = jnp.pad(h, ((0, pad), (0, 0))).reshape(n, blk, d)
    gp = jnp.pad(gates, ((0, pad), (0, 0))).reshape(n, blk, N_EXPERTS)

    def block(args):
        hb, gb = args
        act = jax.nn.silu(jnp.einsum('td,edf->tef', hb, e_w1)) * jnp.einsum('td,edf->tef', hb, e_w3)
        return jnp.einsum('tef,efd->td', act * gb[..., None], e_w2)

    routed = lax.map(block, (hp, gp)).reshape(n * blk, d)[:t]
    shared = (jax.nn.silu(h @ sh_w1) * (h @ sh_w3)) @ sh_w2
    return routed + shared


def decoder_layer(x, c, q_pos, build_blocks, s0, hg_chunk, lb, ada_w, ada_b, n1_g, n2_g, w_in, att_g, hg_g,
                  w_out, r_w, r_b, e_w1, e_w3, e_w2, sh_w1, sh_w3, sh_w2):
    b, t, d = x.shape
    mod = jax.nn.silu(c) @ ada_w + ada_b
    sh1, sc1, g1, sh2, sc2, g2 = jnp.split(mod, 6, axis=-1)
    h = modulate(rmsnorm(x, n1_g), sh1, sc1)
    qa, ka, va, qh, fh, ih, gh = jnp.split(h @ w_in, IN_SPLITS, axis=-1)

    def heads(a, n_h):
        return a.reshape(b, t, n_h, -1)

    qa, ka, va = heads(qa, ATT_HEADS), heads(ka, ATT_HEADS), heads(va, ATT_HEADS)
    kb, vb = build_blocks(ka, va)
    o_att = rmsnorm(moba_attention(qa, kb, vb, q_pos).reshape(b, t, ATT_WIDTH), att_g)
    f = lb + (1.0 - lb) * jax.nn.sigmoid(fh.astype(jnp.float32))
    o_hg, s_new = hgrn2_recurrence(heads(jax.nn.silu(qh), HG_HEADS), heads(jnp.log(f), HG_HEADS),
                                   heads(1.0 - f, HG_HEADS), heads(ih, HG_HEADS), s0, hg_chunk)
    o_hg = (rms_normalize(o_hg) * hg_g.reshape(HG_HEADS, HG_DV).astype(jnp.float32)).reshape(b, t, HG_WIDTH)
    o_hg = (o_hg * jax.nn.silu(gh.astype(jnp.float32))).astype(x.dtype)
    x = x + g1[:, None, :] * (jnp.concatenate([o_att, o_hg], axis=-1) @ w_out)
    h2 = modulate(rmsnorm(x, n2_g), sh2, sc2)
    y = moe_ffn(h2.reshape(b * t, d), r_w, r_b, e_w1, e_w3, e_w2, sh_w1, sh_w3, sh_w2).reshape(b, t, d)
    x = x + g2[:, None, :] * y
    return x, ka, va, s_new


def setup_inputs(seed: int = 0) -> dict:
    key = jax.random.key(seed)
    ks = iter(jax.random.split(key, 40))

    def nrm(shape, s=1.0):
        return jax.random.normal(next(ks), shape, jnp.float32) * s

    def gain(shape):
        return 1.0 + nrm(shape, 0.02)

    n_pages = PAST_LEN // PAGE_SIZE
    in_use = DEC_BATCH * n_pages
    n_phys = in_use + max(1, in_use // 4)
    page_table = jax.random.permutation(next(ks), n_phys)[:in_use].reshape(DEC_BATCH, n_pages).astype(jnp.int32)
    d = D_MODEL
    return {
        'x_prompt': nrm((BATCH, SEQ, d)),
        'x_sample': nrm((DEC_BATCH, DEC_SEQ, d)),
        'c_prompt': nrm((BATCH, d)),
        'c_sample': nrm((DEC_BATCH, d)),
        'cache_k': nrm((DEPTH, n_phys, PAGE_SIZE, ATT_HEADS, ATT_HEAD_DIM)),
        'cache_v': nrm((DEPTH, n_phys, PAGE_SIZE, ATT_HEADS, ATT_HEAD_DIM)),
        'state_hgrn': nrm((DEPTH, DEC_BATCH, HG_HEADS, HG_DK, HG_DV)),
        'page_table': page_table,
        'ada_w': nrm((DEPTH, d, 6 * d), 0.1 * d ** -0.5),
        'ada_b': nrm((DEPTH, 6 * d), 0.1),
        'norm1_g': gain((DEPTH, d)),
        'norm2_g': gain((DEPTH, d)),
        'w_in': nrm((DEPTH, d, IN_COLS), d ** -0.5),
        'att_norm_g': gain((DEPTH, ATT_WIDTH)),
        'hg_norm_g': gain((DEPTH, HG_WIDTH)),
        'hg_lb_logits': nrm((DEPTH + 1, HG_KWIDTH), 0.1),
        'w_out': nrm((DEPTH, MIX_WIDTH, d), MIX_WIDTH ** -0.5),
        'router_w': nrm((DEPTH, d, N_EXPERTS), d ** -0.5),
        'router_bias': nrm((DEPTH, N_EXPERTS), 0.01),
        'exp_w1': nrm((DEPTH, N_EXPERTS, d, EXPERT_DIM), d ** -0.5),
        'exp_w3': nrm((DEPTH, N_EXPERTS, d, EXPERT_DIM), d ** -0.5),
        'exp_w2': nrm((DEPTH, N_EXPERTS, EXPERT_DIM, d), EXPERT_DIM ** -0.5),
        'shared_w1': nrm((DEPTH, d, SHARED_DIM), d ** -0.5),
        'shared_w3': nrm((DEPTH, d, SHARED_DIM), d ** -0.5),
        'shared_w2': nrm((DEPTH, SHARED_DIM, d), SHARED_DIM ** -0.5),
        'final_g': gain((d,)),
        'ada_final_w': nrm((d, 2 * d), 0.1 * d ** -0.5),
        'ada_final_b': nrm((2 * d,), 0.1),
    }


def reference(x_prompt, x_sample, c_prompt, c_sample, cache_k, cache_v, state_hgrn, page_table,
              ada_w, ada_b, norm1_g, norm2_g, w_in, att_norm_g, hg_norm_g, hg_lb_logits, w_out,
              router_w, router_bias, exp_w1, exp_w3, exp_w2, shared_w1, shared_w3, shared_w2,
              final_g, ada_final_w, ada_final_b):
    lb_all = jnp.cumsum(jax.nn.softmax(hg_lb_logits.astype(jnp.float32), axis=0), axis=0)
    pos_prompt = jnp.arange(x_prompt.shape[1], dtype=jnp.int32)
    pos_sample = PAST_LEN + jnp.arange(x_sample.shape[1], dtype=jnp.int32)
    xp, xs = x_prompt, x_sample
    kp_l, vp_l, sp_l, ks_l, vs_l, ss_l = [], [], [], [], [], []
    for l in range(DEPTH):
        w = (ada_w[l], ada_b[l], norm1_g[l], norm2_g[l], w_in[l], att_norm_g[l], hg_norm_g[l], w_out[l],
             router_w[l], router_bias[l], exp_w1[l], exp_w3[l], exp_w2[l], shared_w1[l], shared_w3[l], shared_w2[l])
        s0_prompt = jnp.zeros((xp.shape[0], HG_HEADS, HG_DK, HG_DV), jnp.float32)
        xp, k_new, v_new, s_new = decoder_layer(
            xp, c_prompt, pos_prompt, lambda k, v: (to_blocks(k), to_blocks(v)),
            s0_prompt, HG_CHUNK, lb_all[l], *w)
        kp_l.append(k_new); vp_l.append(v_new); sp_l.append(s_new)
        xs, k_new, v_new, s_new = decoder_layer(
            xs, c_sample, pos_sample,
            lambda k, v, l=l: (paged_blocks(cache_k[l], page_table, k), paged_blocks(cache_v[l], page_table, v)),
            state_hgrn[l], x_sample.shape[1], lb_all[l], *w)
        ks_l.append(k_new); vs_l.append(v_new); ss_l.append(s_new)
    mod_p = jax.nn.silu(c_prompt) @ ada_final_w + ada_final_b
    mod_s = jax.nn.silu(c_sample) @ ada_final_w + ada_final_b
    shift_p, scale_p = jnp.split(mod_p, 2, axis=-1)
    shift_s, scale_s = jnp.split(mod_s, 2, axis=-1)
    y_prompt = modulate(rmsnorm(xp, final_g), shift_p, scale_p)
    y_sample = modulate(rmsnorm(xs, final_g), shift_s, scale_s)
    new_k_prompt = jnp.stack(kp_l)
    new_v_prompt = jnp.stack(vp_l)
    new_state_prompt = jnp.stack(sp_l)
    new_k_sample = jnp.stack(ks_l)
    new_v_sample = jnp.stack(vs_l)
    new_state_sample = jnp.stack(ss_l)
    return (y_prompt, y_sample, new_k_prompt, new_v_prompt, new_state_prompt, new_k_sample, new_v_sample, new_state_sample)
```

```python
import functools

import jax
import jax.numpy as jnp
from jax import lax
from jax.experimental import pallas as pl
from jax.experimental.pallas import tpu as pltpu

F32 = jnp.float32
BF16 = jnp.bfloat16
HI = lax.Precision.HIGHEST

D_MODEL = 1024
PAGE_SIZE = 128
ATT_HEADS = 8
ATT_HEAD_DIM = 64
ATT_WIDTH = ATT_HEADS * ATT_HEAD_DIM
MOBA_BLOCK = 256
MOBA_TOPK = 3
HG_HEADS = 4
HG_DK = 128
HG_DV = 128
HG_WIDTH = HG_HEADS * HG_DV
HG_SUB = 16
N_EXPERTS = 64
EXPERT_TOPK = 6
N_EXPERT_GROUPS = 8
GROUP_SIZE = N_EXPERTS // N_EXPERT_GROUPS
TOPK_GROUPS = 4
EXPERT_DIM = 256
ROUTED_SCALE = 2.5
RMS_EPS = 1e-6
NEG_INF = -1e30
TAKEN = -3e38
LANES = 128
VMEM_LIMIT = 56 * 1024 * 1024


def _silu(x):
    return x * jax.nn.sigmoid(x)


def _rms(x):
    return x * lax.rsqrt(jnp.mean(x * x, axis=-1, keepdims=True) + RMS_EPS)


def _dot_nt(a, b, **kw):
    return lax.dot_general(a, b, (((1,), (1,)), ((), ())), preferred_element_type=F32, **kw)


def _dot_tn(a, b, **kw):
    return lax.dot_general(a, b, (((0,), (0,)), ((), ())), preferred_element_type=F32, **kw)


def _dot(a, b, **kw):
    return jnp.dot(a, b, preferred_element_type=F32, **kw)


def _params(*sem):
    return pltpu.CompilerParams(dimension_semantics=sem, vmem_limit_bytes=VMEM_LIMIT)


def _mod_spec(arr, tm, tiles_per_batch, grid_rank):
    if arr.ndim == 3:
        if grid_rank == 1:
            return pl.BlockSpec((None, 1, arr.shape[-1]), lambda i: (i // tiles_per_batch, 0, 0))
        return pl.BlockSpec((None, 1, arr.shape[-1]), lambda i, e: (i // tiles_per_batch, 0, 0))
    if grid_rank == 1:
        return pl.BlockSpec((tm, arr.shape[-1]), lambda i: (i, 0))
    return pl.BlockSpec((tm, arr.shape[-1]), lambda i, e: (i, 0))


def _full_spec(arr, grid_rank):
    zeros = (0,) * arr.ndim
    if grid_rank == 1:
        return pl.BlockSpec(arr.shape, lambda i: zeros)
    return pl.BlockSpec(arr.shape, lambda i, e: zeros)


def _ada_kernel(c_ref, w_ref, b_ref, o_ref):
    o_ref[...] = _dot(_silu(c_ref[...]), w_ref[...], precision=HI) + b_ref[...]


def _ada(c, w, b, tn):
    m, d = c.shape
    n = w.shape[1]
    return pl.pallas_call(
        _ada_kernel,
        grid=(n // tn,),
        in_specs=[pl.BlockSpec((m, d), lambda j: (0, 0)),
                  pl.BlockSpec((d, tn), lambda j: (0, j)),
                  pl.BlockSpec((1, tn), lambda j: (0, j))],
        out_specs=pl.BlockSpec((m, tn), lambda j: (0, j)),
        out_shape=jax.ShapeDtypeStruct((m, n), F32),
        compiler_params=_params("parallel"),
        name="ada_mod",
    )(c, w, b.reshape(1, n))


def _inproj_kernel(x_ref, sh_ref, sc_ref, g_ref, w_ref, q_ref, k_ref, v_ref, hg_ref):
    h = _rms(x_ref[...]) * g_ref[...]
    h = (h * (1.0 + sc_ref[...]) + sh_ref[...]).astype(BF16)
    a = ATT_WIDTH
    q_ref[...] = _dot(h, w_ref[:, 0:a])
    k_ref[...] = _dot(h, w_ref[:, a:2 * a])
    v_ref[...] = _dot(h, w_ref[:, 2 * a:3 * a])
    hg_ref[...] = _dot(h, w_ref[:, 3 * a:])


def _inproj(x, sh, sc, g, w_bf, tm, tiles_per_batch):
    n, d = x.shape
    hg_cols = w_bf.shape[1] - 3 * ATT_WIDTH
    row = lambda c: pl.BlockSpec((tm, c), lambda i: (i, 0))
    return pl.pallas_call(
        _inproj_kernel,
        grid=(n // tm,),
        in_specs=[row(d), _mod_spec(sh, tm, tiles_per_batch, 1), _mod_spec(sc, tm, tiles_per_batch, 1),
                  _full_spec(g, 1), _full_spec(w_bf, 1)],
        out_specs=[row(ATT_WIDTH), row(ATT_WIDTH), row(ATT_WIDTH), row(hg_cols)],
        out_shape=[jax.ShapeDtypeStruct((n, ATT_WIDTH), F32)] * 3 + [jax.ShapeDtypeStruct((n, hg_cols), F32)],
        compiler_params=_params("parallel"),
        name="in_proj",
    )(x, sh, sc, g, w_bf)


def _head_slope(head, shape):
    out = jnp.zeros(shape, F32)
    for i in range(ATT_HEADS):
        out = jnp.where(head == i, 2.0 ** (-(i + 1)), out)
    return out


def _top_blocks(gate, blk, n_sel):
    nb = gate.shape[-1]
    sel = jnp.zeros(gate.shape, jnp.bool_)
    for _ in range(n_sel):
        m = jnp.max(gate, axis=-1, keepdims=True)
        idx = jnp.min(jnp.where(gate == m, blk, float(nb)), axis=-1, keepdims=True)
        hit = blk == idx
        sel = jnp.logical_or(sel, hit)
        gate = jnp.where(hit, TAKEN, gate)
    return sel


def _moba_prompt_kernel(q_ref, k_ref, v_ref, o_ref, means_ref, *, nb):
    hp = pl.program_id(1)
    qb = pl.program_id(2)
    bs = MOBA_BLOCK

    @pl.when(qb == 0)
    def _():
        for j in range(nb):
            means_ref[j:j + 1, :] = jnp.mean(k_ref[j * bs:(j + 1) * bs, :], axis=0, keepdims=True)

    q = q_ref[...]
    lane_head = lax.broadcasted_iota(jnp.int32, (1, LANES), 1) // ATT_HEAD_DIM
    blk = lax.broadcasted_iota(jnp.int32, (1, nb), 1).astype(F32)
    qb_f = qb.astype(F32)
    past = blk < qb_f
    row = lax.broadcasted_iota(jnp.int32, (bs, 1), 0)
    col = lax.broadcasted_iota(jnp.int32, (1, bs), 1)
    causal = col <= row
    col_f = col.astype(F32)
    scale = ATT_HEAD_DIM ** -0.5
    own0 = pl.multiple_of(qb * bs, bs)
    k_own = k_ref[pl.ds(own0, bs), :].astype(BF16)
    v_own = v_ref[pl.ds(own0, bs), :].astype(BF16)

    qs, sels, slopes, carry0 = [], [], [], []
    for hh in range(2):
        qh = jnp.where(lane_head == hh, q, 0.0)
        gate = _dot_nt(qh, means_ref[...], precision=HI)
        gate = jnp.where(past, gate, NEG_INF)
        sel = jnp.logical_and(_top_blocks(gate, blk, MOBA_TOPK), past)
        slope = _head_slope(2 * hp + hh, (1, 1))
        qs_h = (qh * scale).astype(BF16)
        s = _dot_nt(qs_h, k_own) + slope * col_f
        s = jnp.where(causal, s, NEG_INF)
        m = jnp.max(s, axis=-1, keepdims=True)
        p = jnp.exp(s - m)
        l = jnp.sum(p, axis=-1, keepdims=True)
        acc = _dot(p.astype(BF16), v_own)
        qs.append(qs_h)
        sels.append(sel.astype(F32))
        slopes.append(slope)
        carry0 += [m, l, acc]

    def body(j, carry):
        j0 = pl.multiple_of(j * bs, bs)
        kj = k_ref[pl.ds(j0, bs), :].astype(BF16)
        vj = v_ref[pl.ds(j0, bs), :].astype(BF16)
        j_f = j.astype(F32)
        col_bias = col_f - float(bs) * (qb_f - j_f)
        out = []
        for hh in range(2):
            m, l, acc = carry[3 * hh:3 * hh + 3]
            picked = jnp.max(jnp.where(blk == j_f, sels[hh], 0.0), axis=-1, keepdims=True) > 0.0
            s = _dot_nt(qs[hh], kj) + slopes[hh] * col_bias
            s = jnp.where(picked, s, NEG_INF)
            m_new = jnp.maximum(m, jnp.max(s, axis=-1, keepdims=True))
            alpha = jnp.exp(m - m_new)
            p = jnp.exp(s - m_new)
            l = alpha * l + jnp.sum(p, axis=-1, keepdims=True)
            acc = alpha * acc + _dot(p.astype(BF16), vj)
            out += [m_new, l, acc]
        return tuple(out)

    res = lax.fori_loop(0, qb, body, tuple(carry0))
    o0 = res[2] / res[1]
    o1 = res[5] / res[4]
    o_ref[...] = jnp.where(lane_head == 0, o0, o1)


def _moba_prompt(q, k, v, batch, seq):
    nb = seq // MOBA_BLOCK
    n_hp = ATT_WIDTH // LANES
    qspec = pl.BlockSpec((MOBA_BLOCK, LANES), lambda b, hp, qb: (b * nb + qb, hp))
    kvspec = pl.BlockSpec((seq, LANES), lambda b, hp, qb: (b, hp))
    return pl.pallas_call(
        functools.partial(_moba_prompt_kernel, nb=nb),
        grid=(batch, n_hp, nb),
        in_specs=[qspec, kvspec, kvspec],
        out_specs=qspec,
        out_shape=jax.ShapeDtypeStruct(q.shape, F32),
        scratch_shapes=[pltpu.VMEM((nb, LANES), F32)],
        compiler_params=_params("parallel", "parallel", "arbitrary"),
        name="moba_prompt",
    )(q, k, v)


def _moba_sample_kernel(pt_ref, q_ref, kn_ref, vn_ref, *rest, pb, nblk, past_len, dec_seq):
    del pt_ref
    n_pages = 2 * pb
    k_pages = rest[:n_pages]
    v_pages = rest[n_pages:2 * n_pages]
    o_ref, means_ref, m_ref, l_ref, oall_ref = rest[2 * n_pages:]
    step = pl.program_id(1)
    nrow = dec_seq * ATT_HEADS
    bs = MOBA_BLOCK
    scale = ATT_HEAD_DIM ** -0.5

    q = q_ref[...]
    qrep = jnp.broadcast_to(q[:, None, :], (dec_seq, ATT_HEADS, ATT_WIDTH)).reshape(nrow, ATT_WIDTH)
    row_head = lax.broadcasted_iota(jnp.int32, (nrow, 1), 0) % ATT_HEADS
    lane_head = lax.broadcasted_iota(jnp.int32, (1, ATT_WIDTH), 1) // ATT_HEAD_DIM
    wt = jnp.where(row_head == lane_head, qrep, 0.0)
    wt_s = (wt * scale).astype(BF16)
    slope = _head_slope(row_head, (nrow, 1))
    col_f = lax.broadcasted_iota(jnp.int32, (1, bs), 1).astype(F32)
    blk = lax.broadcasted_iota(jnp.int32, (1, nblk), 1)

    @pl.when(step == 0)
    def _():
        m_ref[...] = jnp.zeros(m_ref.shape, F32)
        l_ref[...] = jnp.zeros(l_ref.shape, F32)

    for i in range(pb):
        j = step * pb + i
        kb = jnp.concatenate([k_pages[2 * i][...], k_pages[2 * i + 1][...]], axis=0)
        vb = jnp.concatenate([v_pages[2 * i][...], v_pages[2 * i + 1][...]], axis=0)
        means_ref[pl.ds(j, 1), :] = jnp.mean(kb, axis=0, keepdims=True)
        rel = col_f + (j * bs - past_len).astype(F32)
        s = _dot_nt(wt_s, kb.astype(BF16)) + slope * rel
        m = jnp.max(s, axis=-1, keepdims=True)
        p = jnp.exp(s - m)
        l = jnp.sum(p, axis=-1, keepdims=True)
        m_ref[...] = jnp.where(blk == j, m, m_ref[...])
        l_ref[...] = jnp.where(blk == j, l, l_ref[...])
        oall_ref[j] = _dot(p.astype(BF16), vb.astype(BF16))

    @pl.when(step == pl.num_programs(1) - 1)
    def _():
        blk_f = blk.astype(F32)
        gate = _dot_nt(wt, means_ref[...], precision=HI)
        sel = _top_blocks(gate, blk_f, min(MOBA_TOPK, nblk))
        kn = kn_ref[...]
        npad = kn.shape[0]
        kcol = lax.broadcasted_iota(jnp.int32, (1, npad), 1)
        qrow = lax.broadcasted_iota(jnp.int32, (nrow, 1), 0) // ATT_HEADS
        so = _dot_nt(wt * scale, kn) + slope * kcol.astype(F32)
        so = jnp.where(kcol <= qrow, so, NEG_INF)
        m_o = jnp.max(so, axis=-1, keepdims=True)
        p_o = jnp.exp(so - m_o)
        l_o = jnp.sum(p_o, axis=-1, keepdims=True)
        o_o = _dot(p_o, vn_ref[...])
        m_all = m_ref[...]
        m_fin = jnp.maximum(m_o, jnp.max(jnp.where(sel, m_all, NEG_INF), axis=-1, keepdims=True))
        w = jnp.where(sel, jnp.exp(m_all - m_fin), 0.0)
        w_o = jnp.exp(m_o - m_fin)
        l_fin = w_o * l_o + jnp.sum(w * l_ref[...], axis=-1, keepdims=True)
        acc = w_o * o_o
        for jj in range(nblk):
            acc = acc + w[:, jj:jj + 1] * oall_ref[jj]
        res = (acc / l_fin).reshape(dec_seq, ATT_HEADS, ATT_WIDTH)
        head3 = lax.broadcasted_iota(jnp.int32, (1, ATT_HEADS, 1), 1)
        lane3 = lax.broadcasted_iota(jnp.int32, (1, 1, ATT_WIDTH), 2) // ATT_HEAD_DIM
        o_ref[...] = jnp.sum(jnp.where(head3 == lane3, res, 0.0), axis=1)


def _moba_sample(q, k_new, v_new, pool_k, pool_v, page_table, pb):
    db, dec_seq, _ = q.shape
    n_pages = page_table.shape[1]
    past_len = n_pages * PAGE_SIZE
    nblk = past_len // MOBA_BLOCK
    pages_per_step = 2 * pb
    n_steps = n_pages // pages_per_step
    nrow = dec_seq * ATT_HEADS
    npad = k_new.shape[1]

    def page_spec(i):
        return pl.BlockSpec((None, PAGE_SIZE, ATT_WIDTH),
                            lambda b, s, pt: (pt[b * n_pages + s * pages_per_step + i], 0, 0))

    tok = lambda r: pl.BlockSpec((None, r, ATT_WIDTH), lambda b, s, pt: (b, 0, 0))
    grid_spec = pltpu.PrefetchScalarGridSpec(
        num_scalar_prefetch=1,
        grid=(db, n_steps),
        in_specs=[tok(dec_seq), tok(npad), tok(npad)]
        + [page_spec(i) for i in range(pages_per_step)] * 2,
        out_specs=tok(dec_seq),
        scratch_shapes=[pltpu.VMEM((nblk, ATT_WIDTH), F32),
                        pltpu.VMEM((nrow, nblk), F32),
                        pltpu.VMEM((nrow, nblk), F32),
                        pltpu.VMEM((nblk, nrow, ATT_WIDTH), F32)],
    )
    return pl.pallas_call(
        functools.partial(_moba_sample_kernel, pb=pb, nblk=nblk, past_len=past_len, dec_seq=dec_seq),
        grid_spec=grid_spec,
        out_shape=jax.ShapeDtypeStruct((db, dec_seq, ATT_WIDTH), F32),
        compiler_params=_params("parallel", "arbitrary"),
        name="moba_sample",
    )(page_table.reshape(-1), q, k_new, v_new,
      *([pool_k] * pages_per_step), *([pool_v] * pages_per_step))


def _hgrn_kernel(hg_ref, lbl_ref, g_ref, s0_ref, o_ref, sout_ref,
                 st_ref, qa_ref, lf_ref, kk_ref, raw_ref, *, tc, n_valid):
    t = pl.program_id(1)
    w = HG_WIDTH
    c = HG_SUB

    @pl.when(t == 0)
    def _():
        for h in range(HG_HEADS):
            st_ref[h] = s0_ref[h].T

    lbl = lbl_ref[...]
    e = jnp.exp(lbl - jnp.max(lbl, axis=0, keepdims=True))
    lb = e[0:1, :] / jnp.sum(e, axis=0, keepdims=True)
    f = lb + (1.0 - lb) * jax.nn.sigmoid(hg_ref[:, w:2 * w])
    lf = jnp.log(f)
    kk = 1.0 - f
    if n_valid < tc:
        valid = lax.broadcasted_iota(jnp.int32, (tc, 1), 0) < n_valid
        lf = jnp.where(valid, lf, 0.0)
        kk = jnp.where(valid, kk, 0.0)
    qa_ref[...] = _silu(hg_ref[:, 0:w])
    lf_ref[...] = lf
    kk_ref[...] = kk

    row = lax.broadcasted_iota(jnp.int32, (c, 1), 0)
    colc = lax.broadcasted_iota(jnp.int32, (1, c), 1)
    tril = (colc <= row).astype(F32)

    def chunk(ci, carry):
        r0 = pl.multiple_of(ci * c, c)
        for h in range(HG_HEADS):
            cs = slice(h * HG_DK, (h + 1) * HG_DK)
            q = qa_ref[pl.ds(r0, c), cs]
            kc = kk_ref[pl.ds(r0, c), cs]
            v = hg_ref[pl.ds(r0, c), 2 * w + h * HG_DV:2 * w + (h + 1) * HG_DV]
            cum = _dot(tril, lf_ref[pl.ds(r0, c), cs], precision=HI)
            last = cum[c - 1:c, :]
            s_t = st_ref[h]
            o = _dot_nt((q * jnp.exp(cum)).astype(BF16), s_t.astype(BF16))
            attn = jnp.zeros((c, c), F32)
            for s in range(c):
                d = jnp.where(row >= s, cum - cum[s:s + 1, :], NEG_INF)
                a_col = jnp.sum(q * kc[s:s + 1, :] * jnp.exp(d), axis=-1, keepdims=True)
                attn = jnp.where(colc == s, a_col, attn)
            o = o + _dot(attn.astype(BF16), v.astype(BF16))
            kf = kc * jnp.exp(last - cum)
            st_ref[h] = s_t * jnp.exp(last) + _dot_tn(v.astype(BF16), kf.astype(BF16))
            raw_ref[pl.ds(r0, c), cs] = o
        return carry

    lax.fori_loop(0, tc // c, chunk, 0)

    for h in range(HG_HEADS):
        cs = slice(h * HG_DV, (h + 1) * HG_DV)
        gate = _silu(hg_ref[:, 3 * w + h * HG_DV:3 * w + (h + 1) * HG_DV])
        o_ref[:, cs] = _rms(raw_ref[:, cs]) * g_ref[:, cs] * gate

    @pl.when(t == pl.num_programs(1) - 1)
    def _():
        for h in range(HG_HEADS):
            sout_ref[h] = st_ref[h].T


def _hgrn(hg, lb_logits, g, s0, batch, tc, n_valid):
    n = hg.shape[0]
    n_t = n // (batch * tc)
    state_spec = pl.BlockSpec((None, HG_HEADS, HG_DK, HG_DV), lambda b, t: (b, 0, 0, 0))
    return pl.pallas_call(
        functools.partial(_hgrn_kernel, tc=tc, n_valid=n_valid),
        grid=(batch, n_t),
        in_specs=[pl.BlockSpec((tc, hg.shape[1]), lambda b, t: (b * n_t + t, 0)),
                  pl.BlockSpec(lb_logits.shape, lambda b, t: (0, 0)),
                  pl.BlockSpec(g.shape, lambda b, t: (0, 0)),
                  state_spec],
        out_specs=[pl.BlockSpec((tc, HG_WIDTH), lambda b, t: (b * n_t + t, 0)), state_spec],
        out_shape=[jax.ShapeDtypeStruct((n, HG_WIDTH), F32),
                   jax.ShapeDtypeStruct((batch, HG_HEADS, HG_DK, HG_DV), F32)],
        scratch_shapes=[pltpu.VMEM((HG_HEADS, HG_DV, HG_DK), F32)] + [pltpu.VMEM((tc, HG_WIDTH), F32)] * 4,
        compiler_params=_params("parallel", "arbitrary"),
        name="hgrn2",
    )(hg, lb_logits, g, s0)


def _route(h2, rwt, rb):
    tm = h2.shape[0]
    ng, gsz = N_EXPERT_GROUPS, GROUP_SIZE
    scores = jax.nn.sigmoid(_dot_nt(rwt, h2, precision=HI))
    s3 = scores.reshape(ng, gsz, tm)
    b3 = (scores + rb).reshape(ng, gsz, tm)
    sub = lax.broadcasted_iota(jnp.int32, (1, gsz, 1), 1).astype(F32)
    grp = lax.broadcasted_iota(jnp.int32, (ng, 1, 1), 0).astype(F32)
    m1 = jnp.max(b3, axis=1, keepdims=True)
    i1 = jnp.min(jnp.where(b3 == m1, sub, float(gsz)), axis=1, keepdims=True)
    m2 = jnp.max(jnp.where(sub == i1, TAKEN, b3), axis=1, keepdims=True)
    gs = m1 + m2
    beaten = jnp.zeros(gs.shape, F32)
    for g in range(ng):
        other = gs[g:g + 1]
        wins = jnp.logical_or(other > gs, jnp.logical_and(other == gs, grp > float(g)))
        beaten = beaten + wins.astype(F32)
    keep = beaten < float(TOPK_GROUPS)
    cand = jnp.where(keep, b3, NEG_INF)
    eidx = grp * float(gsz) + sub
    sel = jnp.zeros(cand.shape, jnp.bool_)
    for _ in range(EXPERT_TOPK):
        m = jnp.max(jnp.max(cand, axis=1, keepdims=True), axis=0, keepdims=True)
        hit_idx = jnp.where(cand == m, eidx, float(N_EXPERTS))
        idx = jnp.min(jnp.min(hit_idx, axis=1, keepdims=True), axis=0, keepdims=True)
        hit = eidx == idx
        sel = jnp.logical_or(sel, hit)
        cand = jnp.where(hit, TAKEN, cand)
    wsel = jnp.where(sel, s3, 0.0)
    tot = jnp.sum(jnp.sum(wsel, axis=1, keepdims=True), axis=0, keepdims=True)
    return (wsel / tot * ROUTED_SCALE).reshape(N_EXPERTS, tm)


def _outproj_kernel(x_ref, oa_ref, oh_ref, ag_ref, wo_ref, g1_ref, sh2_ref, sc2_ref, n2_ref,
                    rwt_ref, rb_ref, x1_ref, h2_ref, gates_ref):
    oa = (_rms(oa_ref[...]) * ag_ref[...]).astype(BF16)
    y = _dot(oa, wo_ref[0:ATT_WIDTH, :]) + _dot(oh_ref[...].astype(BF16), wo_ref[ATT_WIDTH:, :])
    x1 = x_ref[...] + g1_ref[...] * y
    x1_ref[...] = x1
    h2 = _rms(x1) * n2_ref[...]
    h2 = h2 * (1.0 + sc2_ref[...]) + sh2_ref[...]
    h2_ref[...] = h2.astype(BF16)
    gates_t = _route(h2, rwt_ref[...], rb_ref[...])
    pad = jnp.zeros((LANES - N_EXPERTS, gates_t.shape[1]), F32)
    gates_ref[...] = jnp.concatenate([gates_t, pad], axis=0).T


def _outproj(x, oa, oh, ag, wo_bf, g1, sh2, sc2, n2, rwt, rb, tm, tiles_per_batch):
    n, d = x.shape
    row = lambda c: pl.BlockSpec((tm, c), lambda i: (i, 0))
    mod = lambda a: _mod_spec(a, tm, tiles_per_batch, 1)
    return pl.pallas_call(
        _outproj_kernel,
        grid=(n // tm,),
        in_specs=[row(d), row(ATT_WIDTH), row(HG_WIDTH), _full_spec(ag, 1), _full_spec(wo_bf, 1),
                  mod(g1), mod(sh2), mod(sc2), _full_spec(n2, 1), _full_spec(rwt, 1), _full_spec(rb, 1)],
        out_specs=[row(d), row(d), row(LANES)],
        out_shape=[jax.ShapeDtypeStruct((n, d), F32), jax.ShapeDtypeStruct((n, d), BF16),
                   jax.ShapeDtypeStruct((n, LANES), F32)],
        compiler_params=_params("parallel"),
        name="out_proj_route",
    )(x, oa, oh, ag, wo_bf, g1, sh2, sc2, n2, rwt, rb)


def _moe_kernel(h2_ref, gates_ref, x1_ref, g2_ref, shf_ref, scf_ref, fg_ref,
                w1_ref, w3_ref, w2_ref, sw1_ref, sw3_ref, sw2_ref, y_ref, *, sub):
    e = pl.program_id(1)
    tm = h2_ref.shape[0]
    lane = lax.broadcasted_iota(jnp.int32, (1, LANES), 1)

    @pl.when(e == 0)
    def _():
        for r in range(0, tm, sub):
            h = h2_ref[r:r + sub, :]
            act = _silu(_dot(h, sw1_ref[...])) * _dot(h, sw3_ref[...])
            y_ref[r:r + sub, :] = _dot(act.astype(BF16), sw2_ref[...])

    for r in range(0, tm, sub):
        h = h2_ref[r:r + sub, :]
        gate = jnp.sum(jnp.where(lane == e, gates_ref[r:r + sub, :], 0.0), axis=-1, keepdims=True)
        act = _silu(_dot(h, w1_ref[...])) * _dot(h, w3_ref[...]) * gate
        y_ref[r:r + sub, :] += _dot(act.astype(BF16), w2_ref[...])

    @pl.when(e == pl.num_programs(1) - 1)
    def _():
        x2 = x1_ref[...] + g2_ref[...] * y_ref[...]
        y_ref[...] = _rms(x2) * fg_ref[...] * (1.0 + scf_ref[...]) + shf_ref[...]


def _moe(h2, gates, x1, g2, shf, scf, fg, w1, w3, w2, sw1, sw3, sw2, tm, tiles_per_batch):
    n, d = x1.shape
    n_e = w1.shape[0]
    row = lambda c: pl.BlockSpec((tm, c), lambda i, e: (i, 0))
    mod = lambda a: _mod_spec(a, tm, tiles_per_batch, 2)
    wspec = lambda a: pl.BlockSpec((None,) + a.shape[1:], lambda i, e: (e, 0, 0))
    return pl.pallas_call(
        functools.partial(_moe_kernel, sub=min(tm, 256)),
        grid=(n // tm, n_e),
        in_specs=[row(d), row(LANES), row(d), mod(g2), mod(shf), mod(scf), _full_spec(fg, 2),
                  wspec(w1), wspec(w3), wspec(w2), _full_spec(sw1, 2), _full_spec(sw3, 2), _full_spec(sw2, 2)],
        out_specs=row(d),
        out_shape=jax.ShapeDtypeStruct((n, d), F32),
        compiler_params=_params("parallel", "arbitrary"),
        name="moe_final",
    )(h2, gates, x1, g2, shf, scf, fg, w1, w3, w2, sw1, sw3, sw2)


def _split_mod(mod, n_parts, per_token_repeat):
    parts = jnp.split(mod, n_parts, axis=-1)
    if per_token_repeat is None:
        return [p[:, None, :] for p in parts]
    return [jnp.repeat(p, per_token_repeat, axis=0) for p in parts]


def kernel(x_prompt, x_sample, c_prompt, c_sample, cache_k, cache_v, state_hgrn, page_table, ada_w, ada_b, norm1_g, norm2_g, w_in, att_norm_g, hg_norm_g, hg_lb_logits, w_out, router_w, router_bias, exp_w1, exp_w3, exp_w2, shared_w1, shared_w3, shared_w2, final_g, ada_final_w, ada_final_b):
    assert ada_w.shape[0] == 1, "single trunk layer"
    batch, seq, d = x_prompt.shape
    db, dec_seq, _ = x_sample.shape
    n_p, n_s = batch * seq, db * dec_seq

    c_all = jnp.concatenate([c_prompt, c_sample], axis=0)
    c_rows = -(-c_all.shape[0] // 8) * 8
    c_all = jnp.pad(c_all, ((0, c_rows - c_all.shape[0]), (0, 0)))
    mod = _ada(c_all, ada_w[0], ada_b[0], 1024)
    modf = _ada(c_all, ada_final_w, ada_final_b, 1024)
    mods_p = _split_mod(mod[:batch], 6, None) + _split_mod(modf[:batch], 2, None)
    mods_s = _split_mod(mod[batch:batch + db], 6, dec_seq) + _split_mod(modf[batch:batch + db], 2, dec_seq)

    w_in_bf = w_in[0].astype(BF16)
    w_out_bf = w_out[0].astype(BF16)
    w1, w3, w2 = exp_w1[0].astype(BF16), exp_w3[0].astype(BF16), exp_w2[0].astype(BF16)
    sw1, sw3, sw2 = shared_w1[0].astype(BF16), shared_w3[0].astype(BF16), shared_w2[0].astype(BF16)
    n1, n2 = norm1_g[0].reshape(1, d), norm2_g[0].reshape(1, d)
    ag, hgg = att_norm_g[0].reshape(1, ATT_WIDTH), hg_norm_g[0].reshape(1, HG_WIDTH)
    fg = final_g.reshape(1, d)
    rwt = router_w[0].T
    rb = router_bias[0].reshape(N_EXPERTS, 1)
    lbl = hg_lb_logits.astype(F32)

    def tail(x2d, oa, oh, mods, tm, tiles_per_batch, tm_moe, tiles_per_batch_moe):
        sh1, sc1, g1, sh2, sc2, g2, shf, scf = mods
        x1, h2, gates = _outproj(x2d, oa, oh, ag, w_out_bf, g1, sh2, sc2, n2, rwt, rb, tm, tiles_per_batch)
        return _moe(h2, gates, x1, g2, shf, scf, fg, w1, w3, w2, sw1, sw3, sw2, tm_moe, tiles_per_batch_moe)

    xp = x_prompt.reshape(n_p, d)
    tm_p = 512
    q_p, k_p, v_p, hg_p = _inproj(xp, mods_p[0], mods_p[1], n1, w_in_bf, tm_p, seq // tm_p)
    oa_p = _moba_prompt(q_p, k_p, v_p, batch, seq)
    s0_p = jnp.zeros((batch, HG_HEADS, HG_DK, HG_DV), F32)
    oh_p, st_p = _hgrn(hg_p, lbl, hgg, s0_p, batch, 512, 512)
    y_p = tail(xp, oa_p, oh_p, mods_p, tm_p, seq // tm_p, 1024, seq // 1024)

    xs = x_sample.reshape(n_s, d)
    q_s, k_s, v_s, hg_s = _inproj(xs, mods_s[0], mods_s[1], n1, w_in_bf, n_s, 1)
    pad8 = lambda a: jnp.pad(a.reshape(db, dec_seq, ATT_WIDTH), ((0, 0), (0, 8 - dec_seq), (0, 0)))
    n_phys = cache_k.shape[1]
    pool_k = cache_k[0].reshape(n_phys, PAGE_SIZE, ATT_WIDTH)
    pool_v = cache_v[0].reshape(n_phys, PAGE_SIZE, ATT_WIDTH)
    oa_s = _moba_sample(q_s.reshape(db, dec_seq, ATT_WIDTH), pad8(k_s), pad8(v_s), pool_k, pool_v,
                        page_table, 4).reshape(n_s, ATT_WIDTH)
    hg_s_pad = jnp.pad(hg_s.reshape(db, dec_seq, -1), ((0, 0), (0, HG_SUB - dec_seq), (0, 0)))
    oh_s, st_s = _hgrn(hg_s_pad.reshape(db * HG_SUB, -1), lbl, hgg, state_hgrn[0], db, HG_SUB, dec_seq)
    oh_s = oh_s.reshape(db, HG_SUB, HG_WIDTH)[:, :dec_seq].reshape(n_s, HG_WIDTH)
    y_s = tail(xs, oa_s, oh_s, mods_s, n_s, 1, n_s, 1)

    kv_p = (1, batch, seq, ATT_HEADS, ATT_HEAD_DIM)
    kv_s = (1, db, dec_seq, ATT_HEADS, ATT_HEAD_DIM)
    return (y_p.reshape(batch, seq, d), y_s.reshape(db, dec_seq, d),
            k_p.reshape(kv_p), v_p.reshape(kv_p), st_p[None],
            k_s.reshape(kv_s), v_s.reshape(kv_s), st_s[None])
```

```python
import functools

import jax
import jax.numpy as jnp
from jax import lax
from jax.experimental import pallas as pl
from jax.experimental.pallas import tpu as pltpu

F32 = jnp.float32
BF16 = jnp.bfloat16
HI = lax.Precision.HIGHEST

D_MODEL = 1024
PAGE_SIZE = 128
ATT_HEADS = 8
ATT_HEAD_DIM = 64
ATT_WIDTH = ATT_HEADS * ATT_HEAD_DIM
MOBA_BLOCK = 256
MOBA_TOPK = 3
HG_HEADS = 4
HG_DK = 128
HG_DV = 128
HG_WIDTH = HG_HEADS * HG_DV
HG_SUB = 16
N_EXPERTS = 64
EXPERT_TOPK = 6
N_EXPERT_GROUPS = 8
GROUP_SIZE = N_EXPERTS // N_EXPERT_GROUPS
TOPK_GROUPS = 4
EXPERT_DIM = 256
ROUTED_SCALE = 2.5
RMS_EPS = 1e-6
NEG_INF = -1e30
TAKEN = -3e38
LOG2E = 1.4426950408889634
LANES = 128
VMEM_LIMIT = 56 * 1024 * 1024


def _silu(x):
    return x * jax.nn.sigmoid(x)


def _rms(x):
    return x * lax.rsqrt(jnp.mean(x * x, axis=-1, keepdims=True) + RMS_EPS)


def _dot_nt(a, b, **kw):
    return lax.dot_general(a, b, (((1,), (1,)), ((), ())), preferred_element_type=F32, **kw)


def _dot_tn(a, b, **kw):
    return lax.dot_general(a, b, (((0,), (0,)), ((), ())), preferred_element_type=F32, **kw)


def _dot(a, b, **kw):
    return jnp.dot(a, b, preferred_element_type=F32, **kw)


def _params(*sem):
    return pltpu.CompilerParams(dimension_semantics=sem, vmem_limit_bytes=VMEM_LIMIT)


def _mod_spec(arr, tm, tiles_per_batch, grid_rank):
    if arr.ndim == 3:
        if grid_rank == 1:
            return pl.BlockSpec((None, 1, arr.shape[-1]), lambda i: (i // tiles_per_batch, 0, 0))
        return pl.BlockSpec((None, 1, arr.shape[-1]), lambda i, e: (i // tiles_per_batch, 0, 0))
    if grid_rank == 1:
        return pl.BlockSpec((tm, arr.shape[-1]), lambda i: (i, 0))
    return pl.BlockSpec((tm, arr.shape[-1]), lambda i, e: (i, 0))


def _full_spec(arr, grid_rank):
    zeros = (0,) * arr.ndim
    if grid_rank == 1:
        return pl.BlockSpec(arr.shape, lambda i: zeros)
    return pl.BlockSpec(arr.shape, lambda i, e: zeros)


def _ada_kernel(c_ref, w_ref, b_ref, o_ref):
    o_ref[...] = _dot(_silu(c_ref[...]), w_ref[...], precision=HI) + b_ref[...]


def _ada(c, w, b, tn):
    m, d = c.shape
    n = w.shape[1]
    return pl.pallas_call(
        _ada_kernel,
        grid=(n // tn,),
        in_specs=[pl.BlockSpec((m, d), lambda j: (0, 0)),
                  pl.BlockSpec((d, tn), lambda j: (0, j)),
                  pl.BlockSpec((1, tn), lambda j: (0, j))],
        out_specs=pl.BlockSpec((m, tn), lambda j: (0, j)),
        out_shape=jax.ShapeDtypeStruct((m, n), F32),
        compiler_params=_params("parallel"),
        name="ada_mod",
    )(c, w, b.reshape(1, n))


def _inproj_kernel(x_ref, sh_ref, sc_ref, g_ref, w_ref, q_ref, k_ref, v_ref, hg_ref):
    h = _rms(x_ref[...]) * g_ref[...]
    h = (h * (1.0 + sc_ref[...]) + sh_ref[...]).astype(BF16)
    a = ATT_WIDTH
    q_ref[...] = _dot(h, w_ref[:, 0:a])
    k_ref[...] = _dot(h, w_ref[:, a:2 * a])
    v_ref[...] = _dot(h, w_ref[:, 2 * a:3 * a])
    hg_ref[...] = _dot(h, w_ref[:, 3 * a:])


def _inproj(x, sh, sc, g, w_bf, tm, tiles_per_batch):
    n, d = x.shape
    hg_cols = w_bf.shape[1] - 3 * ATT_WIDTH
    row = lambda c: pl.BlockSpec((tm, c), lambda i: (i, 0))
    return pl.pallas_call(
        _inproj_kernel,
        grid=(n // tm,),
        in_specs=[row(d), _mod_spec(sh, tm, tiles_per_batch, 1), _mod_spec(sc, tm, tiles_per_batch, 1),
                  _full_spec(g, 1), _full_spec(w_bf, 1)],
        out_specs=[row(ATT_WIDTH), row(ATT_WIDTH), row(ATT_WIDTH), row(hg_cols)],
        out_shape=[jax.ShapeDtypeStruct((n, ATT_WIDTH), F32)] * 3 + [jax.ShapeDtypeStruct((n, hg_cols), F32)],
        compiler_params=_params("parallel"),
        name="in_proj",
    )(x, sh, sc, g, w_bf)


def _head_slope(head, shape):
    out = jnp.zeros(shape, F32)
    for i in range(ATT_HEADS):
        out = jnp.where(head == i, 2.0 ** (-(i + 1)), out)
    return out


def _top_blocks(gate, blk, n_sel, axis=-1):
    nb = gate.shape[axis]
    sel = jnp.zeros(gate.shape, jnp.bool_)
    for _ in range(n_sel):
        m = jnp.max(gate, axis=axis, keepdims=True)
        idx = jnp.min(jnp.where(gate == m, blk, float(nb)), axis=axis, keepdims=True)
        hit = blk == idx
        sel = jnp.logical_or(sel, hit)
        gate = jnp.where(hit, TAKEN, gate)
    return sel


def _moba_prompt_kernel(q_ref, k_ref, v_ref, o_ref,
                        means_ref, kbf_ref, vt_ref, base_ref, qst_ref, selt_ref, t_ref, *, nb):
    hp = pl.program_id(1)
    qb = pl.program_id(2)
    bs = MOBA_BLOCK
    dh = ATT_HEAD_DIM
    lane_head = lax.broadcasted_iota(jnp.int32, (1, 2 * bs), 1) // bs
    slope2 = _head_slope(2 * hp + lane_head, (1, 2 * bs)) * LOG2E

    @pl.when(qb == 0)
    def _():
        def stage(j, carry):
            j0 = pl.multiple_of(j * bs, bs)
            kj = k_ref[pl.ds(j0, bs), :]
            means_ref[pl.ds(j, 1), :] = jnp.mean(kj, axis=0, keepdims=True)
            kbf_ref[j] = kj.astype(BF16)
            vt_ref[j] = v_ref[pl.ds(j0, bs), :].T.astype(BF16)
            return carry

        lax.fori_loop(0, nb, stage, 0)
        krow_f = lax.broadcasted_iota(jnp.int32, (bs, 2 * bs), 0).astype(F32)
        base_ref[...] = slope2 * krow_f

    qt = q_ref[...].T
    sub_head = lax.broadcasted_iota(jnp.int32, (LANES, 1), 0) // dh
    q2 = jnp.concatenate([jnp.where(sub_head == 0, qt, 0.0), jnp.where(sub_head == 1, qt, 0.0)], axis=1)
    blk = lax.broadcasted_iota(jnp.int32, (nb, 1), 0).astype(F32)
    past = blk < qb.astype(F32)
    gate = jnp.where(past, _dot(means_ref[...], q2, precision=HI), NEG_INF)
    sel = jnp.logical_and(_top_blocks(gate, blk, MOBA_TOPK, axis=0), past)
    selt_ref[...] = sel.astype(F32)
    qst_ref[...] = (q2 * (dh ** -0.5 * LOG2E)).astype(BF16)

    def scores(j, slot):
        t_ref[slot] = _dot(kbf_ref[j], qst_ref[...]) + base_ref[...]

    def weighted_values(j, p):
        vtj = vt_ref[j]
        pb = p.astype(BF16)
        return _dot(vtj[:dh, :], pb[:, :bs]), _dot(vtj[dh:, :], pb[:, bs:])

    scores(qb, 0)
    kq = lax.broadcasted_iota(jnp.int32, (bs, 2 * bs), 1) % bs
    causal = lax.broadcasted_iota(jnp.int32, (bs, 2 * bs), 0) <= kq
    s = jnp.where(causal, t_ref[0], NEG_INF)
    m = jnp.max(s, axis=0, keepdims=True)
    p = jnp.exp2(s - m)
    l = jnp.sum(p, axis=0, keepdims=True)
    acc0, acc1 = weighted_values(qb, p)
    scores(0, 0)

    def attend(j, slot, carry):
        m, l, acc0, acc1 = carry
        jc = jnp.minimum(j, nb - 1)
        c = slope2 * (float(bs) * (qb - j).astype(F32))
        picked = jnp.logical_and(selt_ref[pl.ds(jc, 1), :] > 0.0, j < qb)
        m_blk = jnp.max(t_ref[slot], axis=0, keepdims=True) - c
        m_new = jnp.where(picked, jnp.maximum(m, m_blk), m)
        alpha = jnp.exp2(m - m_new)
        u = jnp.where(picked, m_new + c, -NEG_INF)
        p = jnp.exp2(t_ref[slot] - u)
        l = alpha * l + jnp.sum(p, axis=0, keepdims=True)
        pv0, pv1 = weighted_values(jc, p)
        return m_new, l, alpha[:, :bs] * acc0 + pv0, alpha[:, bs:] * acc1 + pv1

    def body(i, carry):
        j = 2 * i
        scores(jnp.minimum(j + 1, nb - 1), 1)
        carry = attend(j, 0, carry)
        scores(jnp.minimum(j + 2, nb - 1), 0)
        return attend(j + 1, 1, carry)

    m, l, acc0, acc1 = lax.fori_loop(0, (qb + 1) // 2, body, (m, l, acc0, acc1))
    o_t = jnp.concatenate([acc0 / l[:, :bs], acc1 / l[:, bs:]], axis=0)
    o_ref[...] = o_t.T


def _moba_prompt(q, k, v, batch, seq):
    nb = seq // MOBA_BLOCK
    n_hp = ATT_WIDTH // LANES
    qspec = pl.BlockSpec((MOBA_BLOCK, LANES), lambda b, hp, qb: (b * nb + qb, hp))
    kvspec = pl.BlockSpec((seq, LANES), lambda b, hp, qb: (b, hp))
    return pl.pallas_call(
        functools.partial(_moba_prompt_kernel, nb=nb),
        grid=(batch, n_hp, nb),
        in_specs=[qspec, kvspec, kvspec],
        out_specs=qspec,
        out_shape=jax.ShapeDtypeStruct(q.shape, F32),
        scratch_shapes=[pltpu.VMEM((nb, LANES), F32),
                        pltpu.VMEM((nb, MOBA_BLOCK, LANES), BF16),
                        pltpu.VMEM((nb, LANES, MOBA_BLOCK), BF16),
                        pltpu.VMEM((MOBA_BLOCK, 2 * MOBA_BLOCK), F32),
                        pltpu.VMEM((LANES, 2 * MOBA_BLOCK), BF16),
                        pltpu.VMEM((nb, 2 * MOBA_BLOCK), F32),
                        pltpu.VMEM((2, MOBA_BLOCK, 2 * MOBA_BLOCK), F32)],
        compiler_params=_params("parallel", "parallel", "arbitrary"),
        name="moba_prompt",
    )(q, k, v)


def _moba_sample_kernel(pt_ref, q_ref, kn_ref, vn_ref, *rest, pb, nblk, past_len, dec_seq):
    del pt_ref
    n_pages = 2 * pb
    k_pages = rest[:n_pages]
    v_pages = rest[n_pages:2 * n_pages]
    o_ref, means_ref, m_ref, l_ref, oall_ref = rest[2 * n_pages:]
    step = pl.program_id(1)
    nrow = dec_seq * ATT_HEADS
    bs = MOBA_BLOCK
    scale = ATT_HEAD_DIM ** -0.5

    q = q_ref[...]
    qrep = jnp.broadcast_to(q[:, None, :], (dec_seq, ATT_HEADS, ATT_WIDTH)).reshape(nrow, ATT_WIDTH)
    row_head = lax.broadcasted_iota(jnp.int32, (nrow, 1), 0) % ATT_HEADS
    lane_head = lax.broadcasted_iota(jnp.int32, (1, ATT_WIDTH), 1) // ATT_HEAD_DIM
    wt = jnp.where(row_head == lane_head, qrep, 0.0)
    wt_s = (wt * scale).astype(BF16)
    slope = _head_slope(row_head, (nrow, 1))
    col_f = lax.broadcasted_iota(jnp.int32, (1, bs), 1).astype(F32)
    blk = lax.broadcasted_iota(jnp.int32, (1, nblk), 1)

    @pl.when(step == 0)
    def _():
        m_ref[...] = jnp.zeros(m_ref.shape, F32)
        l_ref[...] = jnp.zeros(l_ref.shape, F32)

    for i in range(pb):
        j = step * pb + i
        kb = jnp.concatenate([k_pages[2 * i][...], k_pages[2 * i + 1][...]], axis=0)
        vb = jnp.concatenate([v_pages[2 * i][...], v_pages[2 * i + 1][...]], axis=0)
        means_ref[pl.ds(j, 1), :] = jnp.mean(kb, axis=0, keepdims=True)
        rel = col_f + (j * bs - past_len).astype(F32)
        s = _dot_nt(wt_s, kb.astype(BF16)) + slope * rel
        m = jnp.max(s, axis=-1, keepdims=True)
        p = jnp.exp(s - m)
        l = jnp.sum(p, axis=-1, keepdims=True)
        m_ref[...] = jnp.where(blk == j, m, m_ref[...])
        l_ref[...] = jnp.where(blk == j, l, l_ref[...])
        oall_ref[j] = _dot(p.astype(BF16), vb.astype(BF16))

    @pl.when(step == pl.num_programs(1) - 1)
    def _():
        blk_f = blk.astype(F32)
        gate = _dot_nt(wt, means_ref[...], precision=HI)
        sel = _top_blocks(gate, blk_f, min(MOBA_TOPK, nblk))
        kn = kn_ref[...]
        npad = kn.shape[0]
        kcol = lax.broadcasted_iota(jnp.int32, (1, npad), 1)
        qrow = lax.broadcasted_iota(jnp.int32, (nrow, 1), 0) // ATT_HEADS
        so = _dot_nt(wt * scale, kn) + slope * kcol.astype(F32)
        so = jnp.where(kcol <= qrow, so, NEG_INF)
        m_o = jnp.max(so, axis=-1, keepdims=True)
        p_o = jnp.exp(so - m_o)
        l_o = jnp.sum(p_o, axis=-1, keepdims=True)
        o_o = _dot(p_o, vn_ref[...])
        m_all = m_ref[...]
        m_fin = jnp.maximum(m_o, jnp.max(jnp.where(sel, m_all, NEG_INF), axis=-1, keepdims=True))
        w = jnp.where(sel, jnp.exp(m_all - m_fin), 0.0)
        w_o = jnp.exp(m_o - m_fin)
        l_fin = w_o * l_o + jnp.sum(w * l_ref[...], axis=-1, keepdims=True)
        acc = w_o * o_o
        for jj in range(nblk):
            acc = acc + w[:, jj:jj + 1] * oall_ref[jj]
        res = (acc / l_fin).reshape(dec_seq, ATT_HEADS, ATT_WIDTH)
        head3 = lax.broadcasted_iota(jnp.int32, (1, ATT_HEADS, 1), 1)
        lane3 = lax.broadcasted_iota(jnp.int32, (1, 1, ATT_WIDTH), 2) // ATT_HEAD_DIM
        o_ref[...] = jnp.sum(jnp.where(head3 == lane3, res, 0.0), axis=1)


def _moba_sample(q, k_new, v_new, pool_k, pool_v, page_table, pb):
    db, dec_seq, _ = q.shape
    n_pages = page_table.shape[1]
    past_len = n_pages * PAGE_SIZE
    nblk = past_len // MOBA_BLOCK
    pages_per_step = 2 * pb
    n_steps = n_pages // pages_per_step
    nrow = dec_seq * ATT_HEADS
    npad = k_new.shape[1]

    def page_spec(i):
        return pl.BlockSpec((None, PAGE_SIZE, ATT_WIDTH),
                            lambda b, s, pt: (pt[b * n_pages + s * pages_per_step + i], 0, 0))

    tok = lambda r: pl.BlockSpec((None, r, ATT_WIDTH), lambda b, s, pt: (b, 0, 0))
    grid_spec = pltpu.PrefetchScalarGridSpec(
        num_scalar_prefetch=1,
        grid=(db, n_steps),
        in_specs=[tok(dec_seq), tok(npad), tok(npad)]
        + [page_spec(i) for i in range(pages_per_step)] * 2,
        out_specs=tok(dec_seq),
        scratch_shapes=[pltpu.VMEM((nblk, ATT_WIDTH), F32),
                        pltpu.VMEM((nrow, nblk), F32),
                        pltpu.VMEM((nrow, nblk), F32),
                        pltpu.VMEM((nblk, nrow, ATT_WIDTH), F32)],
    )
    return pl.pallas_call(
        functools.partial(_moba_sample_kernel, pb=pb, nblk=nblk, past_len=past_len, dec_seq=dec_seq),
        grid_spec=grid_spec,
        out_shape=jax.ShapeDtypeStruct((db, dec_seq, ATT_WIDTH), F32),
        compiler_params=_params("parallel", "arbitrary"),
        name="moba_sample",
    )(page_table.reshape(-1), q, k_new, v_new,
      *([pool_k] * pages_per_step), *([pool_v] * pages_per_step))


def _hgrn_kernel(hg_ref, lbl_ref, g_ref, s0_ref, o_ref, sout_ref,
                 st_ref, qa_ref, lf_ref, kk_ref, raw_ref, *, tc, n_valid):
    t = pl.program_id(1)
    w = HG_WIDTH
    c = HG_SUB

    @pl.when(t == 0)
    def _():
        for h in range(HG_HEADS):
            st_ref[h] = s0_ref[h].T

    lbl = lbl_ref[...]
    e = jnp.exp(lbl - jnp.max(lbl, axis=0, keepdims=True))
    lb = e[0:1, :] / jnp.sum(e, axis=0, keepdims=True)
    f = lb + (1.0 - lb) * jax.nn.sigmoid(hg_ref[:, w:2 * w])
    lf = jnp.log(f)
    kk = 1.0 - f
    if n_valid < tc:
        valid = lax.broadcasted_iota(jnp.int32, (tc, 1), 0) < n_valid
        lf = jnp.where(valid, lf, 0.0)
        kk = jnp.where(valid, kk, 0.0)
    qa_ref[...] = _silu(hg_ref[:, 0:w])
    lf_ref[...] = lf
    kk_ref[...] = kk

    row = lax.broadcasted_iota(jnp.int32, (c, 1), 0)
    colc = lax.broadcasted_iota(jnp.int32, (1, c), 1)
    tril = (colc <= row).astype(F32)

    def chunk(ci, carry):
        r0 = pl.multiple_of(ci * c, c)
        for h in range(HG_HEADS):
            cs = slice(h * HG_DK, (h + 1) * HG_DK)
            q = qa_ref[pl.ds(r0, c), cs]
            kc = kk_ref[pl.ds(r0, c), cs]
            v = hg_ref[pl.ds(r0, c), 2 * w + h * HG_DV:2 * w + (h + 1) * HG_DV]
            cum = _dot(tril, lf_ref[pl.ds(r0, c), cs], precision=HI)
            last = cum[c - 1:c, :]
            s_t = st_ref[h]
            o = _dot_nt((q * jnp.exp(cum)).astype(BF16), s_t.astype(BF16))
            attn = jnp.zeros((c, c), F32)
            for s in range(c):
                d = jnp.where(row >= s, cum - cum[s:s + 1, :], NEG_INF)
                a_col = jnp.sum(q * kc[s:s + 1, :] * jnp.exp(d), axis=-1, keepdims=True)
                attn = jnp.where(colc == s, a_col, attn)
            o = o + _dot(attn.astype(BF16), v.astype(BF16))
            kf = kc * jnp.exp(last - cum)
            st_ref[h] = s_t * jnp.exp(last) + _dot_tn(v.astype(BF16), kf.astype(BF16))
            raw_ref[pl.ds(r0, c), cs] = o
        return carry

    lax.fori_loop(0, tc // c, chunk, 0)

    for h in range(HG_HEADS):
        cs = slice(h * HG_DV, (h + 1) * HG_DV)
        gate = _silu(hg_ref[:, 3 * w + h * HG_DV:3 * w + (h + 1) * HG_DV])
        o_ref[:, cs] = _rms(raw_ref[:, cs]) * g_ref[:, cs] * gate

    @pl.when(t == pl.num_programs(1) - 1)
    def _():
        for h in range(HG_HEADS):
            sout_ref[h] = st_ref[h].T


def _hgrn(hg, lb_logits, g, s0, batch, tc, n_valid):
    n = hg.shape[0]
    n_t = n // (batch * tc)
    state_spec = pl.BlockSpec((None, HG_HEADS, HG_DK, HG_DV), lambda b, t: (b, 0, 0, 0))
    return pl.pallas_call(
        functools.partial(_hgrn_kernel, tc=tc, n_valid=n_valid),
        grid=(batch, n_t),
        in_specs=[pl.BlockSpec((tc, hg.shape[1]), lambda b, t: (b * n_t + t, 0)),
                  pl.BlockSpec(lb_logits.shape, lambda b, t: (0, 0)),
                  pl.BlockSpec(g.shape, lambda b, t: (0, 0)),
                  state_spec],
        out_specs=[pl.BlockSpec((tc, HG_WIDTH), lambda b, t: (b * n_t + t, 0)), state_spec],
        out_shape=[jax.ShapeDtypeStruct((n, HG_WIDTH), F32),
                   jax.ShapeDtypeStruct((batch, HG_HEADS, HG_DK, HG_DV), F32)],
        scratch_shapes=[pltpu.VMEM((HG_HEADS, HG_DV, HG_DK), F32)] + [pltpu.VMEM((tc, HG_WIDTH), F32)] * 4,
        compiler_params=_params("parallel", "arbitrary"),
        name="hgrn2",
    )(hg, lb_logits, g, s0)


def _route(h2, rwt, rb):
    tm = h2.shape[0]
    ng, gsz = N_EXPERT_GROUPS, GROUP_SIZE
    scores = jax.nn.sigmoid(_dot_nt(rwt, h2, precision=HI))
    s3 = scores.reshape(ng, gsz, tm)
    b3 = (scores + rb).reshape(ng, gsz, tm)
    sub = lax.broadcasted_iota(jnp.int32, (1, gsz, 1), 1).astype(F32)
    grp = lax.broadcasted_iota(jnp.int32, (ng, 1, 1), 0).astype(F32)
    m1 = jnp.max(b3, axis=1, keepdims=True)
    i1 = jnp.min(jnp.where(b3 == m1, sub, float(gsz)), axis=1, keepdims=True)
    m2 = jnp.max(jnp.where(sub == i1, TAKEN, b3), axis=1, keepdims=True)
    gs = m1 + m2
    beaten = jnp.zeros(gs.shape, F32)
    for g in range(ng):
        other = gs[g:g + 1]
        wins = jnp.logical_or(other > gs, jnp.logical_and(other == gs, grp > float(g)))
        beaten = beaten + wins.astype(F32)
    keep = beaten < float(TOPK_GROUPS)
    cand = jnp.where(keep, b3, NEG_INF)
    eidx = grp * float(gsz) + sub
    sel = jnp.zeros(cand.shape, jnp.bool_)
    for _ in range(EXPERT_TOPK):
        m = jnp.max(jnp.max(cand, axis=1, keepdims=True), axis=0, keepdims=True)
        hit_idx = jnp.where(cand == m, eidx, float(N_EXPERTS))
        idx = jnp.min(jnp.min(hit_idx, axis=1, keepdims=True), axis=0, keepdims=True)
        hit = eidx == idx
        sel = jnp.logical_or(sel, hit)
        cand = jnp.where(hit, TAKEN, cand)
    wsel = jnp.where(sel, s3, 0.0)
    tot = jnp.sum(jnp.sum(wsel, axis=1, keepdims=True), axis=0, keepdims=True)
    return (wsel / tot * ROUTED_SCALE).reshape(N_EXPERTS, tm)


def _outproj_kernel(x_ref, oa_ref, oh_ref, ag_ref, wo_ref, g1_ref, sh2_ref, sc2_ref, n2_ref,
                    rwt_ref, rb_ref, x1_ref, h2_ref, gates_ref):
    oa = (_rms(oa_ref[...]) * ag_ref[...]).astype(BF16)
    y = _dot(oa, wo_ref[0:ATT_WIDTH, :]) + _dot(oh_ref[...].astype(BF16), wo_ref[ATT_WIDTH:, :])
    x1 = x_ref[...] + g1_ref[...] * y
    x1_ref[...] = x1
    h2 = _rms(x1) * n2_ref[...]
    h2 = h2 * (1.0 + sc2_ref[...]) + sh2_ref[...]
    h2_ref[...] = h2.astype(BF16)
    gates_t = _route(h2, rwt_ref[...], rb_ref[...])
    pad = jnp.zeros((LANES - N_EXPERTS, gates_t.shape[1]), F32)
    gates_ref[...] = jnp.concatenate([gates_t, pad], axis=0).T


def _outproj(x, oa, oh, ag, wo_bf, g1, sh2, sc2, n2, rwt, rb, tm, tiles_per_batch):
    n, d = x.shape
    row = lambda c: pl.BlockSpec((tm, c), lambda i: (i, 0))
    mod = lambda a: _mod_spec(a, tm, tiles_per_batch, 1)
    return pl.pallas_call(
        _outproj_kernel,
        grid=(n // tm,),
        in_specs=[row(d), row(ATT_WIDTH), row(HG_WIDTH), _full_spec(ag, 1), _full_spec(wo_bf, 1),
                  mod(g1), mod(sh2), mod(sc2), _full_spec(n2, 1), _full_spec(rwt, 1), _full_spec(rb, 1)],
        out_specs=[row(d), row(d), row(LANES)],
        out_shape=[jax.ShapeDtypeStruct((n, d), F32), jax.ShapeDtypeStruct((n, d), BF16),
                   jax.ShapeDtypeStruct((n, LANES), F32)],
        compiler_params=_params("parallel"),
        name="out_proj_route",
    )(x, oa, oh, ag, wo_bf, g1, sh2, sc2, n2, rwt, rb)


def _moe_kernel(h2_ref, gates_ref, x1_ref, g2_ref, shf_ref, scf_ref, fg_ref,
                w1_ref, w3_ref, w2_ref, sw1_ref, sw3_ref, sw2_ref, y_ref, *, sub):
    e = pl.program_id(1)
    tm = h2_ref.shape[0]
    lane = lax.broadcasted_iota(jnp.int32, (1, LANES), 1)

    @pl.when(e == 0)
    def _():
        for r in range(0, tm, sub):
            h = h2_ref[r:r + sub, :]
            act = _silu(_dot(h, sw1_ref[...])) * _dot(h, sw3_ref[...])
            y_ref[r:r + sub, :] = _dot(act.astype(BF16), sw2_ref[...])

    for r in range(0, tm, sub):
        h = h2_ref[r:r + sub, :]
        gate = jnp.sum(jnp.where(lane == e, gates_ref[r:r + sub, :], 0.0), axis=-1, keepdims=True)
        act = _silu(_dot(h, w1_ref[...])) * _dot(h, w3_ref[...]) * gate
        y_ref[r:r + sub, :] += _dot(act.astype(BF16), w2_ref[...])

    @pl.when(e == pl.num_programs(1) - 1)
    def _():
        x2 = x1_ref[...] + g2_ref[...] * y_ref[...]
        y_ref[...] = _rms(x2) * fg_ref[...] * (1.0 + scf_ref[...]) + shf_ref[...]


def _moe(h2, gates, x1, g2, shf, scf, fg, w1, w3, w2, sw1, sw3, sw2, tm, tiles_per_batch):
    n, d = x1.shape
    n_e = w1.shape[0]
    row = lambda c: pl.BlockSpec((tm, c), lambda i, e: (i, 0))
    mod = lambda a: _mod_spec(a, tm, tiles_per_batch, 2)
    wspec = lambda a: pl.BlockSpec((None,) + a.shape[1:], lambda i, e: (e, 0, 0))
    return pl.pallas_call(
        functools.partial(_moe_kernel, sub=min(tm, 256)),
        grid=(n // tm, n_e),
        in_specs=[row(d), row(LANES), row(d), mod(g2), mod(shf), mod(scf), _full_spec(fg, 2),
                  wspec(w1), wspec(w3), wspec(w2), _full_spec(sw1, 2), _full_spec(sw3, 2), _full_spec(sw2, 2)],
        out_specs=row(d),
        out_shape=jax.ShapeDtypeStruct((n, d), F32),
        compiler_params=_params("parallel", "arbitrary"),
        name="moe_final",
    )(h2, gates, x1, g2, shf, scf, fg, w1, w3, w2, sw1, sw3, sw2)


def _split_mod(mod, n_parts, per_token_repeat):
    parts = jnp.split(mod, n_parts, axis=-1)
    if per_token_repeat is None:
        return [p[:, None, :] for p in parts]
    return [jnp.repeat(p, per_token_repeat, axis=0) for p in parts]


def kernel(x_prompt, x_sample, c_prompt, c_sample, cache_k, cache_v, state_hgrn, page_table, ada_w, ada_b, norm1_g, norm2_g, w_in, att_norm_g, hg_norm_g, hg_lb_logits, w_out, router_w, router_bias, exp_w1, exp_w3, exp_w2, shared_w1, shared_w3, shared_w2, final_g, ada_final_w, ada_final_b):
    assert ada_w.shape[0] == 1, "single trunk layer"
    batch, seq, d = x_prompt.shape
    db, dec_seq, _ = x_sample.shape
    n_p, n_s = batch * seq, db * dec_seq

    c_all = jnp.concatenate([c_prompt, c_sample], axis=0)
    c_rows = -(-c_all.shape[0] // 8) * 8
    c_all = jnp.pad(c_all, ((0, c_rows - c_all.shape[0]), (0, 0)))
    mod = _ada(c_all, ada_w[0], ada_b[0], 1024)
    modf = _ada(c_all, ada_final_w, ada_final_b, 1024)
    mods_p = _split_mod(mod[:batch], 6, None) + _split_mod(modf[:batch], 2, None)
    mods_s = _split_mod(mod[batch:batch + db], 6, dec_seq) + _split_mod(modf[batch:batch + db], 2, dec_seq)

    w_in_bf = w_in[0].astype(BF16)
    w_out_bf = w_out[0].astype(BF16)
    w1, w3, w2 = exp_w1[0].astype(BF16), exp_w3[0].astype(BF16), exp_w2[0].astype(BF16)
    sw1, sw3, sw2 = shared_w1[0].astype(BF16), shared_w3[0].astype(BF16), shared_w2[0].astype(BF16)
    n1, n2 = norm1_g[0].reshape(1, d), norm2_g[0].reshape(1, d)
    ag, hgg = att_norm_g[0].reshape(1, ATT_WIDTH), hg_norm_g[0].reshape(1, HG_WIDTH)
    fg = final_g.reshape(1, d)
    rwt = router_w[0].T
    rb = router_bias[0].reshape(N_EXPERTS, 1)
    lbl = hg_lb_logits.astype(F32)

    def tail(x2d, oa, oh, mods, tm, tiles_per_batch, tm_moe, tiles_per_batch_moe):
        sh1, sc1, g1, sh2, sc2, g2, shf, scf = mods
        x1, h2, gates = _outproj(x2d, oa, oh, ag, w_out_bf, g1, sh2, sc2, n2, rwt, rb, tm, tiles_per_batch)
        return _moe(h2, gates, x1, g2, shf, scf, fg, w1, w3, w2, sw1, sw3, sw2, tm_moe, tiles_per_batch_moe)

    xp = x_prompt.reshape(n_p, d)
    tm_p = 512
    q_p, k_p, v_p, hg_p = _inproj(xp, mods_p[0], mods_p[1], n1, w_in_bf, tm_p, seq // tm_p)
    oa_p = _moba_prompt(q_p, k_p, v_p, batch, seq)
    s0_p = jnp.zeros((batch, HG_HEADS, HG_DK, HG_DV), F32)
    oh_p, st_p = _hgrn(hg_p, lbl, hgg, s0_p, batch, 512, 512)
    y_p = tail(xp, oa_p, oh_p, mods_p, tm_p, seq // tm_p, 1024, seq // 1024)

    xs = x_sample.reshape(n_s, d)
    q_s, k_s, v_s, hg_s = _inproj(xs, mods_s[0], mods_s[1], n1, w_in_bf, n_s, 1)
    pad8 = lambda a: jnp.pad(a.reshape(db, dec_seq, ATT_WIDTH), ((0, 0), (0, 8 - dec_seq), (0, 0)))
    n_phys = cache_k.shape[1]
    pool_k = cache_k[0].reshape(n_phys, PAGE_SIZE, ATT_WIDTH)
    pool_v = cache_v[0].reshape(n_phys, PAGE_SIZE, ATT_WIDTH)
    oa_s = _moba_sample(q_s.reshape(db, dec_seq, ATT_WIDTH), pad8(k_s), pad8(v_s), pool_k, pool_v,
                        page_table, 4).reshape(n_s, ATT_WIDTH)
    hg_s_pad = jnp.pad(hg_s.reshape(db, dec_seq, -1), ((0, 0), (0, HG_SUB - dec_seq), (0, 0)))
    oh_s, st_s = _hgrn(hg_s_pad.reshape(db * HG_SUB, -1), lbl, hgg, state_hgrn[0], db, HG_SUB, dec_seq)
    oh_s = oh_s.reshape(db, HG_SUB, HG_WIDTH)[:, :dec_seq].reshape(n_s, HG_WIDTH)
    y_s = tail(xs, oa_s, oh_s, mods_s, n_s, 1, n_s, 1)

    kv_p = (1, batch, seq, ATT_HEADS, ATT_HEAD_DIM)
    kv_s = (1, db, dec_seq, ATT_HEADS, ATT_HEAD_DIM)
    return (y_p.reshape(batch, seq, d), y_s.reshape(db, dec_seq, d),
            k_p.reshape(kv_p), v_p.reshape(kv_p), st_p[None],
            k_s.reshape(kv_s), v_s.reshape(kv_s), st_s[None])
```

```python
import functools

import jax
import jax.numpy as jnp
from jax import lax
from jax.experimental import pallas as pl
from jax.experimental.pallas import tpu as pltpu

F32 = jnp.float32
BF16 = jnp.bfloat16
HI = lax.Precision.HIGHEST

D_MODEL = 1024
PAGE_SIZE = 128
ATT_HEADS = 8
ATT_HEAD_DIM = 64
ATT_WIDTH = ATT_HEADS * ATT_HEAD_DIM
MOBA_BLOCK = 256
MOBA_TOPK = 3
HG_HEADS = 4
HG_DK = 128
HG_DV = 128
HG_WIDTH = HG_HEADS * HG_DV
HG_SUB = 16
N_EXPERTS = 64
EXPERT_TOPK = 6
N_EXPERT_GROUPS = 8
GROUP_SIZE = N_EXPERTS // N_EXPERT_GROUPS
TOPK_GROUPS = 4
EXPERT_DIM = 256
ROUTED_SCALE = 2.5
RMS_EPS = 1e-6
NEG_INF = -1e30
TAKEN = -3e38
LOG2E = 1.4426950408889634
LANES = 128
VMEM_LIMIT = 56 * 1024 * 1024


def _silu(x):
    return x * jax.nn.sigmoid(x)


def _rms(x):
    return x * lax.rsqrt(jnp.mean(x * x, axis=-1, keepdims=True) + RMS_EPS)


def _dot_nt(a, b, **kw):
    return lax.dot_general(a, b, (((1,), (1,)), ((), ())), preferred_element_type=F32, **kw)


def _dot_tn(a, b, **kw):
    return lax.dot_general(a, b, (((0,), (0,)), ((), ())), preferred_element_type=F32, **kw)


def _dot(a, b, **kw):
    return jnp.dot(a, b, preferred_element_type=F32, **kw)


def _params(*sem):
    return pltpu.CompilerParams(dimension_semantics=sem, vmem_limit_bytes=VMEM_LIMIT)


def _mod_spec(arr, tm, tiles_per_batch, grid_rank):
    if arr.ndim == 3:
        if grid_rank == 1:
            return pl.BlockSpec((None, 1, arr.shape[-1]), lambda i: (i // tiles_per_batch, 0, 0))
        return pl.BlockSpec((None, 1, arr.shape[-1]), lambda i, e: (i // tiles_per_batch, 0, 0))
    if grid_rank == 1:
        return pl.BlockSpec((tm, arr.shape[-1]), lambda i: (i, 0))
    return pl.BlockSpec((tm, arr.shape[-1]), lambda i, e: (i, 0))


def _full_spec(arr, grid_rank):
    zeros = (0,) * arr.ndim
    if grid_rank == 1:
        return pl.BlockSpec(arr.shape, lambda i: zeros)
    return pl.BlockSpec(arr.shape, lambda i, e: zeros)


def _ada_kernel(c_ref, w_ref, b_ref, o_ref):
    o_ref[...] = _dot(_silu(c_ref[...]), w_ref[...], precision=HI) + b_ref[...]


def _ada(c, w, b, tn):
    m, d = c.shape
    n = w.shape[1]
    return pl.pallas_call(
        _ada_kernel,
        grid=(n // tn,),
        in_specs=[pl.BlockSpec((m, d), lambda j: (0, 0)),
                  pl.BlockSpec((d, tn), lambda j: (0, j)),
                  pl.BlockSpec((1, tn), lambda j: (0, j))],
        out_specs=pl.BlockSpec((m, tn), lambda j: (0, j)),
        out_shape=jax.ShapeDtypeStruct((m, n), F32),
        compiler_params=_params("parallel"),
        name="ada_mod",
    )(c, w, b.reshape(1, n))


def _inproj_kernel(x_ref, sh_ref, sc_ref, g_ref, w_ref, q_ref, k_ref, v_ref, hg_ref):
    h = _rms(x_ref[...]) * g_ref[...]
    h = (h * (1.0 + sc_ref[...]) + sh_ref[...]).astype(BF16)
    a = ATT_WIDTH
    q_ref[...] = _dot(h, w_ref[:, 0:a])
    k_ref[...] = _dot(h, w_ref[:, a:2 * a])
    v_ref[...] = _dot(h, w_ref[:, 2 * a:3 * a])
    hg_ref[...] = _dot(h, w_ref[:, 3 * a:])


def _inproj(x, sh, sc, g, w_bf, tm, tiles_per_batch):
    n, d = x.shape
    hg_cols = w_bf.shape[1] - 3 * ATT_WIDTH
    row = lambda c: pl.BlockSpec((tm, c), lambda i: (i, 0))
    return pl.pallas_call(
        _inproj_kernel,
        grid=(n // tm,),
        in_specs=[row(d), _mod_spec(sh, tm, tiles_per_batch, 1), _mod_spec(sc, tm, tiles_per_batch, 1),
                  _full_spec(g, 1), _full_spec(w_bf, 1)],
        out_specs=[row(ATT_WIDTH), row(ATT_WIDTH), row(ATT_WIDTH), row(hg_cols)],
        out_shape=[jax.ShapeDtypeStruct((n, ATT_WIDTH), F32)] * 3 + [jax.ShapeDtypeStruct((n, hg_cols), F32)],
        compiler_params=_params("parallel"),
        name="in_proj",
    )(x, sh, sc, g, w_bf)


def _head_slope(head, shape):
    out = jnp.zeros(shape, F32)
    for i in range(ATT_HEADS):
        out = jnp.where(head == i, 2.0 ** (-(i + 1)), out)
    return out


def _top_blocks(gate, blk, n_sel, axis=-1):
    nb = gate.shape[axis]
    sel = jnp.zeros(gate.shape, jnp.bool_)
    for _ in range(n_sel):
        m = jnp.max(gate, axis=axis, keepdims=True)
        idx = jnp.min(jnp.where(gate == m, blk, float(nb)), axis=axis, keepdims=True)
        hit = blk == idx
        sel = jnp.logical_or(sel, hit)
        gate = jnp.where(hit, TAKEN, gate)
    return sel


def _moba_prompt_kernel(q_ref, k_ref, v_ref, o_ref,
                        means_ref, kbf_ref, vt_ref, base_ref, qst_ref, selt_ref, t_ref, *, nb):
    hp = pl.program_id(1)
    qb = pl.program_id(2)
    bs = MOBA_BLOCK
    dh = ATT_HEAD_DIM
    lane_head = lax.broadcasted_iota(jnp.int32, (1, 2 * bs), 1) // bs
    slope2 = _head_slope(2 * hp + lane_head, (1, 2 * bs)) * LOG2E

    @pl.when(qb == 0)
    def _():
        def stage(j, carry):
            j0 = pl.multiple_of(j * bs, bs)
            kj = k_ref[pl.ds(j0, bs), :]
            means_ref[pl.ds(j, 1), :] = jnp.mean(kj, axis=0, keepdims=True)
            kbf_ref[j] = kj.astype(BF16)
            vt_ref[j] = v_ref[pl.ds(j0, bs), :].T.astype(BF16)
            return carry

        lax.fori_loop(0, nb, stage, 0)
        krow_f = lax.broadcasted_iota(jnp.int32, (bs, 2 * bs), 0).astype(F32)
        base_ref[...] = slope2 * krow_f

    qt = q_ref[...].T
    sub_head = lax.broadcasted_iota(jnp.int32, (LANES, 1), 0) // dh
    q2 = jnp.concatenate([jnp.where(sub_head == 0, qt, 0.0), jnp.where(sub_head == 1, qt, 0.0)], axis=1)
    blk = lax.broadcasted_iota(jnp.int32, (nb, 1), 0).astype(F32)
    past = blk < qb.astype(F32)
    gate = jnp.where(past, _dot(means_ref[...], q2, precision=HI), NEG_INF)
    sel = jnp.logical_and(_top_blocks(gate, blk, MOBA_TOPK, axis=0), past)
    selt_ref[...] = sel.astype(F32)
    qst_ref[...] = (q2 * (dh ** -0.5 * LOG2E)).astype(BF16)

    def scores(j, slot):
        t_ref[slot] = _dot(kbf_ref[j], qst_ref[...]) + base_ref[...]

    def weighted_values(j, p):
        vtj = vt_ref[j]
        pb = p.astype(BF16)
        return _dot(vtj[:dh, :], pb[:, :bs]), _dot(vtj[dh:, :], pb[:, bs:])

    scores(qb, 0)
    kq = lax.broadcasted_iota(jnp.int32, (bs, 2 * bs), 1) % bs
    causal = lax.broadcasted_iota(jnp.int32, (bs, 2 * bs), 0) <= kq
    s = jnp.where(causal, t_ref[0], NEG_INF)
    m = jnp.max(s, axis=0, keepdims=True)
    p = jnp.exp2(s - m)
    l = jnp.sum(p, axis=0, keepdims=True)
    acc0, acc1 = weighted_values(qb, p)
    scores(0, 0)

    def attend(j, slot, carry):
        m, l, acc0, acc1 = carry
        jc = jnp.minimum(j, nb - 1)
        c = slope2 * (float(bs) * (qb - j).astype(F32))
        picked = jnp.logical_and(selt_ref[pl.ds(jc, 1), :] > 0.0, j < qb)
        m_blk = jnp.max(t_ref[slot], axis=0, keepdims=True) - c
        m_new = jnp.where(picked, jnp.maximum(m, m_blk), m)
        alpha = jnp.exp2(m - m_new)
        u = jnp.where(picked, m_new + c, -NEG_INF)
        p = jnp.exp2(t_ref[slot] - u)
        l = alpha * l + jnp.sum(p, axis=0, keepdims=True)
        pv0, pv1 = weighted_values(jc, p)
        return m_new, l, alpha[:, :bs] * acc0 + pv0, alpha[:, bs:] * acc1 + pv1

    def body(i, carry):
        j = 2 * i
        scores(jnp.minimum(j + 1, nb - 1), 1)
        carry = attend(j, 0, carry)
        scores(jnp.minimum(j + 2, nb - 1), 0)
        return attend(j + 1, 1, carry)

    m, l, acc0, acc1 = lax.fori_loop(0, (qb + 1) // 2, body, (m, l, acc0, acc1))
    o_t = jnp.concatenate([acc0 / l[:, :bs], acc1 / l[:, bs:]], axis=0)
    o_ref[...] = o_t.T


def _moba_prompt(q, k, v, batch, seq):
    nb = seq // MOBA_BLOCK
    n_hp = ATT_WIDTH // LANES
    qspec = pl.BlockSpec((MOBA_BLOCK, LANES), lambda b, hp, qb: (b * nb + qb, hp))
    kvspec = pl.BlockSpec((seq, LANES), lambda b, hp, qb: (b, hp))
    return pl.pallas_call(
        functools.partial(_moba_prompt_kernel, nb=nb),
        grid=(batch, n_hp, nb),
        in_specs=[qspec, kvspec, kvspec],
        out_specs=qspec,
        out_shape=jax.ShapeDtypeStruct(q.shape, F32),
        scratch_shapes=[pltpu.VMEM((nb, LANES), F32),
                        pltpu.VMEM((nb, MOBA_BLOCK, LANES), BF16),
                        pltpu.VMEM((nb, LANES, MOBA_BLOCK), BF16),
                        pltpu.VMEM((MOBA_BLOCK, 2 * MOBA_BLOCK), F32),
                        pltpu.VMEM((LANES, 2 * MOBA_BLOCK), BF16),
                        pltpu.VMEM((nb, 2 * MOBA_BLOCK), F32),
                        pltpu.VMEM((2, MOBA_BLOCK, 2 * MOBA_BLOCK), F32)],
        compiler_params=_params("parallel", "parallel", "arbitrary"),
        name="moba_prompt",
    )(q, k, v)


def _pool_means_kernel(pt_ref, *refs, n_blk_step):
    del pt_ref
    pages = refs[:2 * n_blk_step]
    o_ref = refs[2 * n_blk_step]
    for i in range(n_blk_step):
        tot = jnp.sum(pages[2 * i][...], axis=0) + jnp.sum(pages[2 * i + 1][...], axis=0)
        o_ref[i] = tot * (1.0 / MOBA_BLOCK)


def _pool_means(pool_k, page_table, blocks_per_step):
    db, n_pages = page_table.shape
    nblk = n_pages * PAGE_SIZE // MOBA_BLOCK
    pps = 2 * blocks_per_step

    def page_spec(i):
        return pl.BlockSpec((None, PAGE_SIZE, ATT_HEADS, ATT_HEAD_DIM),
                            lambda b, s, pt: (pt[b * n_pages + s * pps + i], 0, 0, 0))

    grid_spec = pltpu.PrefetchScalarGridSpec(
        num_scalar_prefetch=1,
        grid=(db, n_pages // pps),
        in_specs=[page_spec(i) for i in range(pps)],
        out_specs=pl.BlockSpec((None, blocks_per_step, ATT_HEADS, ATT_HEAD_DIM), lambda b, s, pt: (b, s, 0, 0)),
    )
    return pl.pallas_call(
        functools.partial(_pool_means_kernel, n_blk_step=blocks_per_step),
        grid_spec=grid_spec,
        out_shape=jax.ShapeDtypeStruct((db, nblk, ATT_HEADS, ATT_HEAD_DIM), F32),
        compiler_params=_params("parallel", "parallel"),
        name="pool_means",
    )(page_table.reshape(-1), *([pool_k] * pps))


def _select_kernel(q_ref, means_ref, idx_ref, *, n_sel):
    means = means_ref[...]
    nblk = means.shape[0]
    blk = lax.broadcasted_iota(jnp.int32, (nblk, 1, 1), 0).astype(F32)
    for i in range(q_ref.shape[0]):
        gate = jnp.sum(means * q_ref[i][None], axis=-1, keepdims=True)
        for r in range(n_sel):
            m = jnp.max(gate, axis=0, keepdims=True)
            idx = jnp.min(jnp.where(gate == m, blk, float(nblk)), axis=0, keepdims=True)
            idx_ref[i, r] = idx[0].astype(jnp.int32)
            gate = jnp.where(blk == idx, TAKEN, gate)


def _select_blocks(q4, means, n_sel):
    db, dec_seq = q4.shape[:2]
    nblk = means.shape[1]
    return pl.pallas_call(
        functools.partial(_select_kernel, n_sel=n_sel),
        grid=(db,),
        in_specs=[pl.BlockSpec((None, dec_seq, ATT_HEADS, ATT_HEAD_DIM), lambda b: (b, 0, 0, 0)),
                  pl.BlockSpec((None, nblk, ATT_HEADS, ATT_HEAD_DIM), lambda b: (b, 0, 0, 0))],
        out_specs=pl.BlockSpec((None, dec_seq, n_sel, ATT_HEADS, 1), lambda b: (b, 0, 0, 0, 0)),
        out_shape=jax.ShapeDtypeStruct((db, dec_seq, n_sel, ATT_HEADS, 1), jnp.int32),
        compiler_params=_params("parallel"),
        name="moba_select",
    )(q4, means)


def _gather_attn_kernel(pt_ref, idx_ref, q_ref, kn_ref, vn_ref, pk_ref, pv_ref, o_ref, kbuf, vbuf, sem,
                        *, n_pages, dec_seq, n_sel):
    i = pl.program_id(1)
    step = pl.program_id(0) * dec_seq + i
    n_steps = pl.num_programs(0) * dec_seq
    slot = step % 2
    past_len = n_pages * PAGE_SIZE
    rows_per_page = PAGE_SIZE * ATT_HEADS
    pages_per_block = MOBA_BLOCK // PAGE_SIZE

    def picked_block(stp, h, r):
        return idx_ref[(stp * n_sel + r) * ATT_HEADS + h]

    def copies(stp, slt):
        bb = stp // dec_seq
        out = []
        for h in range(ATT_HEADS):
            for r in range(n_sel):
                blk = picked_block(stp, h, r)
                for pp in range(pages_per_block):
                    page = pt_ref[bb * n_pages + blk * pages_per_block + pp]
                    dst = pl.ds(pp * PAGE_SIZE, PAGE_SIZE)
                    c = h * n_sel + r
                    out.append(pltpu.make_async_copy(pk_ref.at[page, :, h, :], kbuf.at[slt, c, dst, :], sem.at[slt]))
                    out.append(pltpu.make_async_copy(pv_ref.at[page, :, h, :], vbuf.at[slt, c, dst, :], sem.at[slt]))
        return out

    @pl.when(step == 0)
    def _():
        for cp in copies(step, slot):
            cp.start()

    @pl.when(step + 1 < n_steps)
    def _():
        for cp in copies(step + 1, 1 - slot):
            cp.start()

    for cp in copies(step, slot):
        cp.wait()

    qv = q_ref[...] * (ATT_HEAD_DIM ** -0.5)
    row_f = lax.broadcasted_iota(jnp.int32, (MOBA_BLOCK, 1), 0).astype(F32)
    own = lax.broadcasted_iota(jnp.int32, (kn_ref.shape[1], 1), 0)
    outs = []
    for h in range(ATT_HEADS):
        qh = qv[h:h + 1, :]
        slope = 2.0 ** (-(h + 1))
        so = jnp.sum(kn_ref[h] * qh, axis=-1, keepdims=True) + slope * own.astype(F32)
        so = jnp.where(own <= i, so, NEG_INF)
        s_blocks = []
        m = jnp.max(so, axis=0, keepdims=True)
        for r in range(n_sel):
            rel = (picked_block(step, h, r) * MOBA_BLOCK - past_len).astype(F32)
            s = jnp.sum(kbuf[slot, h * n_sel + r] * qh, axis=-1, keepdims=True) + slope * (row_f + rel)
            s_blocks.append(s)
            m = jnp.maximum(m, jnp.max(s, axis=0, keepdims=True))
        p = jnp.exp(so - m)
        l = jnp.sum(p, axis=0, keepdims=True)
        acc = jnp.sum(p * vn_ref[h], axis=0, keepdims=True)
        for r in range(n_sel):
            p = jnp.exp(s_blocks[r] - m)
            l = l + jnp.sum(p, axis=0, keepdims=True)
            acc = acc + jnp.sum(p * vbuf[slot, h * n_sel + r], axis=0, keepdims=True)
        outs.append(acc / l)
    o_ref[...] = jnp.concatenate(outs, axis=0)


def _moba_sample(q, k_new, v_new, cache_k, cache_v, page_table):
    db, dec_seq, _ = q.shape
    n_phys = cache_k.shape[0]
    n_pages = page_table.shape[1]
    nblk = n_pages * PAGE_SIZE // MOBA_BLOCK
    n_sel = min(MOBA_TOPK, nblk)
    n_combo = ATT_HEADS * n_sel
    heads4 = lambda a: a.reshape(db, dec_seq, ATT_HEADS, ATT_HEAD_DIM)
    q4 = heads4(q)
    means = _pool_means(cache_k, page_table, 8)
    idx = _select_blocks(q4, means, n_sel)
    new = lambda a: jnp.pad(heads4(a).transpose(0, 2, 1, 3), ((0, 0), (0, 0), (0, 8 - dec_seq), (0, 0)))
    rows = lambda c: c.reshape(n_phys * PAGE_SIZE * ATT_HEADS, ATT_HEAD_DIM)
    qspec = pl.BlockSpec((None, None, ATT_HEADS, ATT_HEAD_DIM), lambda b, i, pt, ix: (b, i, 0, 0))
    nspec = pl.BlockSpec((None, ATT_HEADS, 8, ATT_HEAD_DIM), lambda b, i, pt, ix: (b, 0, 0, 0))
    grid_spec = pltpu.PrefetchScalarGridSpec(
        num_scalar_prefetch=2,
        grid=(db, dec_seq),
        in_specs=[qspec, nspec, nspec, pl.BlockSpec(memory_space=pl.ANY), pl.BlockSpec(memory_space=pl.ANY)],
        out_specs=qspec,
        scratch_shapes=[pltpu.VMEM((2, n_combo, MOBA_BLOCK, ATT_HEAD_DIM), F32),
                        pltpu.VMEM((2, n_combo, MOBA_BLOCK, ATT_HEAD_DIM), F32),
                        pltpu.SemaphoreType.DMA((2,))],
    )
    out = pl.pallas_call(
        functools.partial(_gather_attn_kernel, n_pages=n_pages, dec_seq=dec_seq, n_sel=n_sel),
        grid_spec=grid_spec,
        out_shape=jax.ShapeDtypeStruct((db, dec_seq, ATT_HEADS, ATT_HEAD_DIM), F32),
        compiler_params=_params("arbitrary", "arbitrary"),
        name="moba_sample",
    )(page_table.reshape(-1), idx.reshape(-1), q4, new(k_new), new(v_new), cache_k, cache_v)
    return out.reshape(db * dec_seq, ATT_WIDTH)


def _hgrn_kernel(hg_ref, lbl_ref, g_ref, s0_ref, o_ref, sout_ref,
                 st_ref, qa_ref, cum_ref, kk_ref, raw_ref, qf_ref, kf_ref, vb_ref, *, tc, n_valid):
    t = pl.program_id(1)
    w = HG_WIDTH
    c = HG_SUB
    nc = tc // c
    slab = min(tc, LANES)

    @pl.when(t == 0)
    def _():
        for h in range(HG_HEADS):
            st_ref[h] = s0_ref[h].T

    lbl = lbl_ref[...]
    e = jnp.exp(lbl - jnp.max(lbl, axis=0, keepdims=True))
    lb = e[0:1, :] / jnp.sum(e, axis=0, keepdims=True)
    f = lb + (1.0 - lb) * jax.nn.sigmoid(hg_ref[:, w:2 * w])
    lf = jnp.log(f)
    kk = 1.0 - f
    if n_valid < tc:
        valid = lax.broadcasted_iota(jnp.int32, (tc, 1), 0) < n_valid
        lf = jnp.where(valid, lf, 0.0)
        kk = jnp.where(valid, kk, 0.0)
    row = lax.broadcasted_iota(jnp.int32, (tc, 1), 0)
    pos = row % c
    cum = lf
    shift = 1
    while shift < c:
        cum = cum + jnp.where(pos >= shift, pltpu.roll(cum, shift, axis=0), 0.0)
        shift *= 2
    q = _silu(hg_ref[:, 0:w])
    last = jnp.broadcast_to(cum.reshape(nc, c, w)[:, c - 1:c, :], (nc, c, w)).reshape(tc, w)
    qa_ref[...] = q
    kk_ref[...] = kk
    cum_ref[...] = cum
    qf_ref[...] = (q * jnp.exp(cum)).astype(BF16)
    kf_ref[...] = (kk * jnp.exp(last - cum)).astype(BF16)
    vb_ref[...] = hg_ref[:, 2 * w:3 * w].astype(BF16)

    chunk_col = (row % slab) // c * c
    lane = lax.broadcasted_iota(jnp.int32, (1, slab), 1)
    sub = lax.broadcasted_iota(jnp.int32, (1, c, 1), 1)
    for h in range(HG_HEADS):
        cs = slice(h * HG_DK, (h + 1) * HG_DK)
        q3 = qa_ref[:, cs].reshape(nc, c, HG_DK)
        k3 = kk_ref[:, cs].reshape(nc, c, HG_DK)
        cum3 = cum_ref[:, cs].reshape(nc, c, HG_DK)
        attn = jnp.zeros((tc, slab), F32)
        for s in range(c):
            d = jnp.where(sub >= s, cum3 - cum3[:, s:s + 1, :], NEG_INF)
            a_col = jnp.sum(q3 * k3[:, s:s + 1, :] * jnp.exp(d), axis=-1, keepdims=True)
            attn = jnp.where(lane == chunk_col + s, a_col.reshape(tc, 1), attn)
        for g in range(tc // slab):
            rows = slice(g * slab, (g + 1) * slab)
            raw_ref[rows, cs] = _dot(attn[rows].astype(BF16), vb_ref[rows, cs])

    def chunk(ci, carry):
        r0 = pl.multiple_of(ci * c, c)
        for h in range(HG_HEADS):
            cs = slice(h * HG_DK, (h + 1) * HG_DK)
            s_t = st_ref[h]
            decay = jnp.exp(cum_ref[pl.ds(r0, c), cs][c - 1:c, :])
            raw_ref[pl.ds(r0, c), cs] += _dot_nt(qf_ref[pl.ds(r0, c), cs], s_t.astype(BF16))
            st_ref[h] = s_t * decay + _dot_tn(vb_ref[pl.ds(r0, c), cs], kf_ref[pl.ds(r0, c), cs])
        return carry

    lax.fori_loop(0, nc, chunk, 0)

    for h in range(HG_HEADS):
        cs = slice(h * HG_DV, (h + 1) * HG_DV)
        gate = _silu(hg_ref[:, 3 * w + h * HG_DV:3 * w + (h + 1) * HG_DV])
        o_ref[:, cs] = _rms(raw_ref[:, cs]) * g_ref[:, cs] * gate

    @pl.when(t == pl.num_programs(1) - 1)
    def _():
        for h in range(HG_HEADS):
            sout_ref[h] = st_ref[h].T


def _hgrn(hg, lb_logits, g, s0, batch, tc, n_valid):
    n = hg.shape[0]
    n_t = n // (batch * tc)
    state_spec = pl.BlockSpec((None, HG_HEADS, HG_DK, HG_DV), lambda b, t: (b, 0, 0, 0))
    return pl.pallas_call(
        functools.partial(_hgrn_kernel, tc=tc, n_valid=n_valid),
        grid=(batch, n_t),
        in_specs=[pl.BlockSpec((tc, hg.shape[1]), lambda b, t: (b * n_t + t, 0)),
                  pl.BlockSpec(lb_logits.shape, lambda b, t: (0, 0)),
                  pl.BlockSpec(g.shape, lambda b, t: (0, 0)),
                  state_spec],
        out_specs=[pl.BlockSpec((tc, HG_WIDTH), lambda b, t: (b * n_t + t, 0)), state_spec],
        out_shape=[jax.ShapeDtypeStruct((n, HG_WIDTH), F32),
                   jax.ShapeDtypeStruct((batch, HG_HEADS, HG_DK, HG_DV), F32)],
        scratch_shapes=[pltpu.VMEM((HG_HEADS, HG_DV, HG_DK), F32)]
        + [pltpu.VMEM((tc, HG_WIDTH), F32)] * 4 + [pltpu.VMEM((tc, HG_WIDTH), BF16)] * 3,
        compiler_params=_params("parallel", "arbitrary"),
        name="hgrn2",
    )(hg, lb_logits, g, s0)


def _route(h2, rwt, rb):
    tm = h2.shape[0]
    ng, gsz = N_EXPERT_GROUPS, GROUP_SIZE
    scores = jax.nn.sigmoid(_dot_nt(rwt, h2, precision=HI))
    s3 = scores.reshape(ng, gsz, tm)
    b3 = (scores + rb).reshape(ng, gsz, tm)
    sub = lax.broadcasted_iota(jnp.int32, (1, gsz, 1), 1).astype(F32)
    grp = lax.broadcasted_iota(jnp.int32, (ng, 1, 1), 0).astype(F32)
    m1 = jnp.max(b3, axis=1, keepdims=True)
    i1 = jnp.min(jnp.where(b3 == m1, sub, float(gsz)), axis=1, keepdims=True)
    m2 = jnp.max(jnp.where(sub == i1, TAKEN, b3), axis=1, keepdims=True)
    gs = m1 + m2
    beaten = jnp.zeros(gs.shape, F32)
    for g in range(ng):
        other = gs[g:g + 1]
        wins = jnp.logical_or(other > gs, jnp.logical_and(other == gs, grp > float(g)))
        beaten = beaten + wins.astype(F32)
    keep = beaten < float(TOPK_GROUPS)
    cand = jnp.where(keep, b3, NEG_INF)
    eidx = grp * float(gsz) + sub
    sel = jnp.zeros(cand.shape, jnp.bool_)
    for _ in range(EXPERT_TOPK):
        m = jnp.max(jnp.max(cand, axis=1, keepdims=True), axis=0, keepdims=True)
        hit_idx = jnp.where(cand == m, eidx, float(N_EXPERTS))
        idx = jnp.min(jnp.min(hit_idx, axis=1, keepdims=True), axis=0, keepdims=True)
        hit = eidx == idx
        sel = jnp.logical_or(sel, hit)
        cand = jnp.where(hit, TAKEN, cand)
    wsel = jnp.where(sel, s3, 0.0)
    tot = jnp.sum(jnp.sum(wsel, axis=1, keepdims=True), axis=0, keepdims=True)
    return (wsel / tot * ROUTED_SCALE).reshape(N_EXPERTS, tm)


def _outproj_kernel(x_ref, oa_ref, oh_ref, ag_ref, wo_ref, g1_ref, sh2_ref, sc2_ref, n2_ref,
                    rwt_ref, rb_ref, x1_ref, h2_ref, gates_ref):
    oa = (_rms(oa_ref[...]) * ag_ref[...]).astype(BF16)
    y = _dot(oa, wo_ref[0:ATT_WIDTH, :]) + _dot(oh_ref[...].astype(BF16), wo_ref[ATT_WIDTH:, :])
    x1 = x_ref[...] + g1_ref[...] * y
    x1_ref[...] = x1
    h2 = _rms(x1) * n2_ref[...]
    h2 = h2 * (1.0 + sc2_ref[...]) + sh2_ref[...]
    h2_ref[...] = h2.astype(BF16)
    gates_t = _route(h2, rwt_ref[...], rb_ref[...])
    pad = jnp.zeros((LANES - N_EXPERTS, gates_t.shape[1]), F32)
    gates_ref[...] = jnp.concatenate([gates_t, pad], axis=0).T


def _outproj(x, oa, oh, ag, wo_bf, g1, sh2, sc2, n2, rwt, rb, tm, tiles_per_batch):
    n, d = x.shape
    row = lambda c: pl.BlockSpec((tm, c), lambda i: (i, 0))
    mod = lambda a: _mod_spec(a, tm, tiles_per_batch, 1)
    return pl.pallas_call(
        _outproj_kernel,
        grid=(n // tm,),
        in_specs=[row(d), row(ATT_WIDTH), row(HG_WIDTH), _full_spec(ag, 1), _full_spec(wo_bf, 1),
                  mod(g1), mod(sh2), mod(sc2), _full_spec(n2, 1), _full_spec(rwt, 1), _full_spec(rb, 1)],
        out_specs=[row(d), row(d), row(LANES)],
        out_shape=[jax.ShapeDtypeStruct((n, d), F32), jax.ShapeDtypeStruct((n, d), BF16),
                   jax.ShapeDtypeStruct((n, LANES), F32)],
        compiler_params=_params("parallel"),
        name="out_proj_route",
    )(x, oa, oh, ag, wo_bf, g1, sh2, sc2, n2, rwt, rb)


def _moe_kernel(h2_ref, gates_ref, x1_ref, g2_ref, shf_ref, scf_ref, fg_ref,
                w1_ref, w3_ref, w2_ref, sw1_ref, sw3_ref, sw2_ref, y_ref, *, sub):
    e = pl.program_id(1)
    tm = h2_ref.shape[0]
    lane = lax.broadcasted_iota(jnp.int32, (1, LANES), 1)

    @pl.when(e == 0)
    def _():
        for r in range(0, tm, sub):
            h = h2_ref[r:r + sub, :]
            act = _silu(_dot(h, sw1_ref[...])) * _dot(h, sw3_ref[...])
            y_ref[r:r + sub, :] = _dot(act.astype(BF16), sw2_ref[...])

    for r in range(0, tm, sub):
        h = h2_ref[r:r + sub, :]
        gate = jnp.sum(jnp.where(lane == e, gates_ref[r:r + sub, :], 0.0), axis=-1, keepdims=True)
        act = _silu(_dot(h, w1_ref[...])) * _dot(h, w3_ref[...]) * gate
        y_ref[r:r + sub, :] += _dot(act.astype(BF16), w2_ref[...])

    @pl.when(e == pl.num_programs(1) - 1)
    def _():
        x2 = x1_ref[...] + g2_ref[...] * y_ref[...]
        y_ref[...] = _rms(x2) * fg_ref[...] * (1.0 + scf_ref[...]) + shf_ref[...]


def _moe(h2, gates, x1, g2, shf, scf, fg, w1, w3, w2, sw1, sw3, sw2, tm, tiles_per_batch):
    n, d = x1.shape
    n_e = w1.shape[0]
    row = lambda c: pl.BlockSpec((tm, c), lambda i, e: (i, 0))
    mod = lambda a: _mod_spec(a, tm, tiles_per_batch, 2)
    wspec = lambda a: pl.BlockSpec((None,) + a.shape[1:], lambda i, e: (e, 0, 0))
    return pl.pallas_call(
        functools.partial(_moe_kernel, sub=min(tm, 256)),
        grid=(n // tm, n_e),
        in_specs=[row(d), row(LANES), row(d), mod(g2), mod(shf), mod(scf), _full_spec(fg, 2),
                  wspec(w1), wspec(w3), wspec(w2), _full_spec(sw1, 2), _full_spec(sw3, 2), _full_spec(sw2, 2)],
        out_specs=row(d),
        out_shape=jax.ShapeDtypeStruct((n, d), F32),
        compiler_params=_params("parallel", "arbitrary"),
        name="moe_final",
    )(h2, gates, x1, g2, shf, scf, fg, w1, w3, w2, sw1, sw3, sw2)


def _split_mod(mod, n_parts, per_token_repeat):
    parts = jnp.split(mod, n_parts, axis=-1)
    if per_token_repeat is None:
        return [p[:, None, :] for p in parts]
    return [jnp.repeat(p, per_token_repeat, axis=0) for p in parts]


def kernel(x_prompt, x_sample, c_prompt, c_sample, cache_k, cache_v, state_hgrn, page_table, ada_w, ada_b, norm1_g, norm2_g, w_in, att_norm_g, hg_norm_g, hg_lb_logits, w_out, router_w, router_bias, exp_w1, exp_w3, exp_w2, shared_w1, shared_w3, shared_w2, final_g, ada_final_w, ada_final_b):
    assert ada_w.shape[0] == 1, "single trunk layer"
    batch, seq, d = x_prompt.shape
    db, dec_seq, _ = x_sample.shape
    n_p, n_s = batch * seq, db * dec_seq

    c_all = jnp.concatenate([c_prompt, c_sample], axis=0)
    c_rows = -(-c_all.shape[0] // 8) * 8
    c_all = jnp.pad(c_all, ((0, c_rows - c_all.shape[0]), (0, 0)))
    mod = _ada(c_all, ada_w[0], ada_b[0], 1024)
    modf = _ada(c_all, ada_final_w, ada_final_b, 1024)
    mods_p = _split_mod(mod[:batch], 6, None) + _split_mod(modf[:batch], 2, None)
    mods_s = _split_mod(mod[batch:batch + db], 6, dec_seq) + _split_mod(modf[batch:batch + db], 2, dec_seq)

    w_in_bf = w_in[0].astype(BF16)
    w_out_bf = w_out[0].astype(BF16)
    w1, w3, w2 = exp_w1[0].astype(BF16), exp_w3[0].astype(BF16), exp_w2[0].astype(BF16)
    sw1, sw3, sw2 = shared_w1[0].astype(BF16), shared_w3[0].astype(BF16), shared_w2[0].astype(BF16)
    n1, n2 = norm1_g[0].reshape(1, d), norm2_g[0].reshape(1, d)
    ag, hgg = att_norm_g[0].reshape(1, ATT_WIDTH), hg_norm_g[0].reshape(1, HG_WIDTH)
    fg = final_g.reshape(1, d)
    rwt = router_w[0].T
    rb = router_bias[0].reshape(N_EXPERTS, 1)
    lbl = hg_lb_logits.astype(F32)

    def tail(x2d, oa, oh, mods, tm, tiles_per_batch, tm_moe, tiles_per_batch_moe):
        sh1, sc1, g1, sh2, sc2, g2, shf, scf = mods
        x1, h2, gates = _outproj(x2d, oa, oh, ag, w_out_bf, g1, sh2, sc2, n2, rwt, rb, tm, tiles_per_batch)
        return _moe(h2, gates, x1, g2, shf, scf, fg, w1, w3, w2, sw1, sw3, sw2, tm_moe, tiles_per_batch_moe)

    xp = x_prompt.reshape(n_p, d)
    tm_p = 512
    q_p, k_p, v_p, hg_p = _inproj(xp, mods_p[0], mods_p[1], n1, w_in_bf, tm_p, seq // tm_p)
    oa_p = _moba_prompt(q_p, k_p, v_p, batch, seq)
    s0_p = jnp.zeros((batch, HG_HEADS, HG_DK, HG_DV), F32)
    oh_p, st_p = _hgrn(hg_p, lbl, hgg, s0_p, batch, 512, 512)
    y_p = tail(xp, oa_p, oh_p, mods_p, tm_p, seq // tm_p, 1024, seq // 1024)

    xs = x_sample.reshape(n_s, d)
    q_s, k_s, v_s, hg_s = _inproj(xs, mods_s[0], mods_s[1], n1, w_in_bf, n_s, 1)
    tok3 = lambda a: a.reshape(db, dec_seq, ATT_WIDTH)
    oa_s = _moba_sample(tok3(q_s), tok3(k_s), tok3(v_s), cache_k[0], cache_v[0], page_table)
    hg_s_pad = jnp.pad(hg_s.reshape(db, dec_seq, -1), ((0, 0), (0, HG_SUB - dec_seq), (0, 0)))
    oh_s, st_s = _hgrn(hg_s_pad.reshape(db * HG_SUB, -1), lbl, hgg, state_hgrn[0], db, HG_SUB, dec_seq)
    oh_s = oh_s.reshape(db, HG_SUB, HG_WIDTH)[:, :dec_seq].reshape(n_s, HG_WIDTH)
    y_s = tail(xs, oa_s, oh_s, mods_s, n_s, 1, n_s, 1)

    kv_p = (1, batch, seq, ATT_HEADS, ATT_HEAD_DIM)
    kv_s = (1, db, dec_seq, ATT_HEADS, ATT_HEAD_DIM)
    return (y_p.reshape(batch, seq, d), y_s.reshape(db, dec_seq, d),
            k_p.reshape(kv_p), v_p.reshape(kv_p), st_p[None],
            k_s.reshape(kv_s), v_s.reshape(kv_s), st_s[None])
```

```python
import functools

import jax
import jax.numpy as jnp
from jax import lax
from jax.experimental import pallas as pl
from jax.experimental.pallas import tpu as pltpu

F32 = jnp.float32
BF16 = jnp.bfloat16
HI = lax.Precision.HIGHEST

D_MODEL = 1024
PAGE_SIZE = 128
ATT_HEADS = 8
ATT_HEAD_DIM = 64
ATT_WIDTH = ATT_HEADS * ATT_HEAD_DIM
MOBA_BLOCK = 256
MOBA_TOPK = 3
HG_HEADS = 4
HG_DK = 128
HG_DV = 128
HG_WIDTH = HG_HEADS * HG_DV
HG_SUB = 16
N_EXPERTS = 64
EXPERT_TOPK = 6
N_EXPERT_GROUPS = 8
GROUP_SIZE = N_EXPERTS // N_EXPERT_GROUPS
TOPK_GROUPS = 4
EXPERT_DIM = 256
ROUTED_SCALE = 2.5
RMS_EPS = 1e-6
NEG_INF = -1e30
TAKEN = -3e38
LOG2E = 1.4426950408889634
LANES = 128
VMEM_LIMIT = 56 * 1024 * 1024


def _silu(x):
    return x * jax.nn.sigmoid(x)


def _rms(x):
    return x * lax.rsqrt(jnp.mean(x * x, axis=-1, keepdims=True) + RMS_EPS)


def _dot_nt(a, b, **kw):
    return lax.dot_general(a, b, (((1,), (1,)), ((), ())), preferred_element_type=F32, **kw)


def _dot_tn(a, b, **kw):
    return lax.dot_general(a, b, (((0,), (0,)), ((), ())), preferred_element_type=F32, **kw)


def _dot(a, b, **kw):
    return jnp.dot(a, b, preferred_element_type=F32, **kw)


def _params(*sem):
    return pltpu.CompilerParams(dimension_semantics=sem, vmem_limit_bytes=VMEM_LIMIT)


def _mod_spec(arr, tm, tiles_per_batch, grid_rank):
    if arr.ndim == 3:
        if grid_rank == 1:
            return pl.BlockSpec((None, 1, arr.shape[-1]), lambda i: (i // tiles_per_batch, 0, 0))
        return pl.BlockSpec((None, 1, arr.shape[-1]), lambda i, e: (i // tiles_per_batch, 0, 0))
    if grid_rank == 1:
        return pl.BlockSpec((tm, arr.shape[-1]), lambda i: (i, 0))
    return pl.BlockSpec((tm, arr.shape[-1]), lambda i, e: (i, 0))


def _full_spec(arr, grid_rank):
    zeros = (0,) * arr.ndim
    if grid_rank == 1:
        return pl.BlockSpec(arr.shape, lambda i: zeros)
    return pl.BlockSpec(arr.shape, lambda i, e: zeros)


def _ada_kernel(c_ref, w_ref, b_ref, o_ref):
    o_ref[...] = _dot(_silu(c_ref[...]), w_ref[...], precision=HI) + b_ref[...]


def _ada(c, w, b, tn):
    m, d = c.shape
    n = w.shape[1]
    return pl.pallas_call(
        _ada_kernel,
        grid=(n // tn,),
        in_specs=[pl.BlockSpec((m, d), lambda j: (0, 0)),
                  pl.BlockSpec((d, tn), lambda j: (0, j)),
                  pl.BlockSpec((1, tn), lambda j: (0, j))],
        out_specs=pl.BlockSpec((m, tn), lambda j: (0, j)),
        out_shape=jax.ShapeDtypeStruct((m, n), F32),
        compiler_params=_params("parallel"),
        name="ada_mod",
    )(c, w, b.reshape(1, n))


def _inproj_kernel(x_ref, sh_ref, sc_ref, g_ref, w_ref, q_ref, k_ref, v_ref, hg_ref):
    h = _rms(x_ref[...]) * g_ref[...]
    h = (h * (1.0 + sc_ref[...]) + sh_ref[...]).astype(BF16)
    a = ATT_WIDTH
    q_ref[...] = _dot(h, w_ref[:, 0:a])
    k_ref[...] = _dot(h, w_ref[:, a:2 * a])
    v_ref[...] = _dot(h, w_ref[:, 2 * a:3 * a])
    hg_ref[...] = _dot(h, w_ref[:, 3 * a:])


def _inproj(x, sh, sc, g, w_bf, tm, tiles_per_batch):
    n, d = x.shape
    hg_cols = w_bf.shape[1] - 3 * ATT_WIDTH
    row = lambda c: pl.BlockSpec((tm, c), lambda i: (i, 0))
    return pl.pallas_call(
        _inproj_kernel,
        grid=(n // tm,),
        in_specs=[row(d), _mod_spec(sh, tm, tiles_per_batch, 1), _mod_spec(sc, tm, tiles_per_batch, 1),
                  _full_spec(g, 1), _full_spec(w_bf, 1)],
        out_specs=[row(ATT_WIDTH), row(ATT_WIDTH), row(ATT_WIDTH), row(hg_cols)],
        out_shape=[jax.ShapeDtypeStruct((n, ATT_WIDTH), F32)] * 3 + [jax.ShapeDtypeStruct((n, hg_cols), F32)],
        compiler_params=_params("parallel"),
        name="in_proj",
    )(x, sh, sc, g, w_bf)


def _head_slope(head, shape):
    out = jnp.zeros(shape, F32)
    for i in range(ATT_HEADS):
        out = jnp.where(head == i, 2.0 ** (-(i + 1)), out)
    return out


def _top_blocks(gate, blk, n_sel, axis=-1):
    nb = gate.shape[axis]
    sel = jnp.zeros(gate.shape, jnp.bool_)
    for _ in range(n_sel):
        m = jnp.max(gate, axis=axis, keepdims=True)
        idx = jnp.min(jnp.where(gate == m, blk, float(nb)), axis=axis, keepdims=True)
        hit = blk == idx
        sel = jnp.logical_or(sel, hit)
        gate = jnp.where(hit, TAKEN, gate)
    return sel


def _moba_prompt_kernel(q_ref, k_ref, v_ref, o_ref,
                        means_ref, kbf_ref, vt_ref, base_ref, qst_ref, selt_ref, t_ref, *, nb):
    hp = pl.program_id(1)
    qb = pl.program_id(2)
    bs = MOBA_BLOCK
    dh = ATT_HEAD_DIM
    lane_head = lax.broadcasted_iota(jnp.int32, (1, 2 * bs), 1) // bs
    slope2 = _head_slope(2 * hp + lane_head, (1, 2 * bs)) * LOG2E

    @pl.when(qb == 0)
    def _():
        def stage(j, carry):
            j0 = pl.multiple_of(j * bs, bs)
            kj = k_ref[pl.ds(j0, bs), :]
            means_ref[pl.ds(j, 1), :] = jnp.mean(kj, axis=0, keepdims=True)
            kbf_ref[j] = kj.astype(BF16)
            vt_ref[j] = v_ref[pl.ds(j0, bs), :].T.astype(BF16)
            return carry

        lax.fori_loop(0, nb, stage, 0)
        krow_f = lax.broadcasted_iota(jnp.int32, (bs, 2 * bs), 0).astype(F32)
        base_ref[...] = slope2 * krow_f

    qt = q_ref[...].T
    sub_head = lax.broadcasted_iota(jnp.int32, (LANES, 1), 0) // dh
    q2 = jnp.concatenate([jnp.where(sub_head == 0, qt, 0.0), jnp.where(sub_head == 1, qt, 0.0)], axis=1)
    blk = lax.broadcasted_iota(jnp.int32, (nb, 1), 0).astype(F32)
    past = blk < qb.astype(F32)
    gate = jnp.where(past, _dot(means_ref[...], q2, precision=HI), NEG_INF)
    sel = jnp.logical_and(_top_blocks(gate, blk, MOBA_TOPK, axis=0), past)
    selt_ref[...] = sel.astype(F32)
    qst_ref[...] = (q2 * (dh ** -0.5 * LOG2E)).astype(BF16)

    def scores(j, slot):
        t_ref[slot] = _dot(kbf_ref[j], qst_ref[...]) + base_ref[...]

    def weighted_values(j, p):
        vtj = vt_ref[j]
        pb = p.astype(BF16)
        return _dot(vtj[:dh, :], pb[:, :bs]), _dot(vtj[dh:, :], pb[:, bs:])

    scores(qb, 0)
    kq = lax.broadcasted_iota(jnp.int32, (bs, 2 * bs), 1) % bs
    causal = lax.broadcasted_iota(jnp.int32, (bs, 2 * bs), 0) <= kq
    s = jnp.where(causal, t_ref[0], NEG_INF)
    m = jnp.max(s, axis=0, keepdims=True)
    p = jnp.exp2(s - m)
    l = jnp.sum(p, axis=0, keepdims=True)
    acc0, acc1 = weighted_values(qb, p)
    scores(0, 0)

    def attend(j, slot, carry):
        m, l, acc0, acc1 = carry
        jc = jnp.minimum(j, nb - 1)
        c = slope2 * (float(bs) * (qb - j).astype(F32))
        picked = jnp.logical_and(selt_ref[pl.ds(jc, 1), :] > 0.0, j < qb)
        m_blk = jnp.max(t_ref[slot], axis=0, keepdims=True) - c
        m_new = jnp.where(picked, jnp.maximum(m, m_blk), m)
        alpha = jnp.exp2(m - m_new)
        u = jnp.where(picked, m_new + c, -NEG_INF)
        p = jnp.exp2(t_ref[slot] - u)
        l = alpha * l + jnp.sum(p, axis=0, keepdims=True)
        pv0, pv1 = weighted_values(jc, p)
        return m_new, l, alpha[:, :bs] * acc0 + pv0, alpha[:, bs:] * acc1 + pv1

    def body(i, carry):
        j = 2 * i
        scores(jnp.minimum(j + 1, nb - 1), 1)
        carry = attend(j, 0, carry)
        scores(jnp.minimum(j + 2, nb - 1), 0)
        return attend(j + 1, 1, carry)

    m, l, acc0, acc1 = lax.fori_loop(0, (qb + 1) // 2, body, (m, l, acc0, acc1))
    o_t = jnp.concatenate([acc0 / l[:, :bs], acc1 / l[:, bs:]], axis=0)
    o_ref[...] = o_t.T


def _moba_prompt(q, k, v, batch, seq):
    nb = seq // MOBA_BLOCK
    n_hp = ATT_WIDTH // LANES
    qspec = pl.BlockSpec((MOBA_BLOCK, LANES), lambda b, hp, qb: (b * nb + qb, hp))
    kvspec = pl.BlockSpec((seq, LANES), lambda b, hp, qb: (b, hp))
    return pl.pallas_call(
        functools.partial(_moba_prompt_kernel, nb=nb),
        grid=(batch, n_hp, nb),
        in_specs=[qspec, kvspec, kvspec],
        out_specs=qspec,
        out_shape=jax.ShapeDtypeStruct(q.shape, F32),
        scratch_shapes=[pltpu.VMEM((nb, LANES), F32),
                        pltpu.VMEM((nb, MOBA_BLOCK, LANES), BF16),
                        pltpu.VMEM((nb, LANES, MOBA_BLOCK), BF16),
                        pltpu.VMEM((MOBA_BLOCK, 2 * MOBA_BLOCK), F32),
                        pltpu.VMEM((LANES, 2 * MOBA_BLOCK), BF16),
                        pltpu.VMEM((nb, 2 * MOBA_BLOCK), F32),
                        pltpu.VMEM((2, MOBA_BLOCK, 2 * MOBA_BLOCK), F32)],
        compiler_params=_params("parallel", "parallel", "arbitrary"),
        name="moba_prompt",
    )(q, k, v)


def _pool_means_kernel(pt_ref, *refs, n_blk_step):
    del pt_ref
    pages = refs[:2 * n_blk_step]
    o_ref = refs[2 * n_blk_step]
    s = pl.program_id(1)
    lane = lax.broadcasted_iota(jnp.int32, (1, o_ref.shape[1]), 1)

    @pl.when(s == 0)
    def _():
        o_ref[...] = jnp.zeros(o_ref.shape, F32)

    acc = o_ref[...]
    for i in range(n_blk_step):
        both = pages[2 * i][...] + pages[2 * i + 1][...]
        col = jnp.sum(both, axis=-1, keepdims=True).reshape(ATT_WIDTH, 1) * (1.0 / MOBA_BLOCK)
        acc = jnp.where(lane == s * n_blk_step + i, col, acc)
    o_ref[...] = acc


def _pool_means(pool_kt, page_table, blocks_per_step):
    db, n_pages = page_table.shape
    nblk = n_pages * PAGE_SIZE // MOBA_BLOCK
    pps = 2 * blocks_per_step

    def page_spec(i):
        return pl.BlockSpec((None, ATT_HEADS, ATT_HEAD_DIM, PAGE_SIZE),
                            lambda b, s, pt: (pt[b * n_pages + s * pps + i], 0, 0, 0))

    grid_spec = pltpu.PrefetchScalarGridSpec(
        num_scalar_prefetch=1,
        grid=(db, n_pages // pps),
        in_specs=[page_spec(i) for i in range(pps)],
        out_specs=pl.BlockSpec((None, ATT_WIDTH, nblk), lambda b, s, pt: (b, 0, 0)),
    )
    return pl.pallas_call(
        functools.partial(_pool_means_kernel, n_blk_step=blocks_per_step),
        grid_spec=grid_spec,
        out_shape=jax.ShapeDtypeStruct((db, ATT_WIDTH, nblk), F32),
        compiler_params=_params("parallel", "arbitrary"),
        name="pool_means",
    )(page_table.reshape(-1), *([pool_kt] * pps))


def _select_kernel(q_ref, mt_ref, idx_ref, *, n_sel):
    q = q_ref[...]
    dec_seq = q.shape[0]
    nrow = dec_seq * ATT_HEADS
    qrep = jnp.broadcast_to(q[:, None, :], (dec_seq, ATT_HEADS, ATT_WIDTH)).reshape(nrow, ATT_WIDTH)
    row_head = lax.broadcasted_iota(jnp.int32, (nrow, 1), 0) % ATT_HEADS
    lane_head = lax.broadcasted_iota(jnp.int32, (1, ATT_WIDTH), 1) // ATT_HEAD_DIM
    gate = _dot(jnp.where(row_head == lane_head, qrep, 0.0), mt_ref[...], precision=HI)
    nblk = gate.shape[1]
    blk = lax.broadcasted_iota(jnp.int32, (1, nblk), 1).astype(F32)
    pick = lax.broadcasted_iota(jnp.int32, (1, n_sel), 1)
    out = jnp.zeros((nrow, n_sel), F32)
    for r in range(n_sel):
        m = jnp.max(gate, axis=-1, keepdims=True)
        idx = jnp.min(jnp.where(gate == m, blk, float(nblk)), axis=-1, keepdims=True)
        out = jnp.where(pick == r, idx, out)
        gate = jnp.where(blk == idx, TAKEN, gate)
    idx_ref[...] = out.astype(jnp.int32)


def _select_blocks(q3, means_t, n_sel):
    db, dec_seq, _ = q3.shape
    nblk = means_t.shape[2]
    nrow = dec_seq * ATT_HEADS
    return pl.pallas_call(
        functools.partial(_select_kernel, n_sel=n_sel),
        grid=(db,),
        in_specs=[pl.BlockSpec((None, dec_seq, ATT_WIDTH), lambda b: (b, 0, 0)),
                  pl.BlockSpec((None, ATT_WIDTH, nblk), lambda b: (b, 0, 0))],
        out_specs=pl.BlockSpec((None, nrow, n_sel), lambda b: (b, 0, 0)),
        out_shape=jax.ShapeDtypeStruct((db, nrow, n_sel), jnp.int32),
        compiler_params=_params("parallel"),
        name="moba_select",
    )(q3, means_t)


def _gather_attn_kernel(pt_ref, idx_ref, q_ref, kn_ref, vn_ref, pk_ref, pv_ref, o_ref, kbuf, vbuf, sem,
                        *, n_pages, dec_seq, n_sel):
    i = pl.program_id(1)
    step = pl.program_id(0) * dec_seq + i
    n_steps = pl.num_programs(0) * dec_seq
    slot = step % 2
    past_len = n_pages * PAGE_SIZE
    pages_per_block = MOBA_BLOCK // PAGE_SIZE
    dh = ATT_HEAD_DIM

    def picked_block(stp, h, r):
        return idx_ref[(stp * ATT_HEADS + h) * n_sel + r]

    def copies(stp, slt):
        bb = stp // dec_seq
        out = []
        for h in range(ATT_HEADS):
            for r in range(n_sel):
                blk = picked_block(stp, h, r)
                for pp in range(pages_per_block):
                    page = pt_ref[bb * n_pages + blk * pages_per_block + pp]
                    dst = pl.ds(pp * PAGE_SIZE, PAGE_SIZE)
                    c = h * n_sel + r
                    out.append(pltpu.make_async_copy(pk_ref.at[page, h], kbuf.at[slt, c, :, dst], sem.at[slt]))
                    out.append(pltpu.make_async_copy(pv_ref.at[page, h], vbuf.at[slt, c, :, dst], sem.at[slt]))
        return out

    @pl.when(step == 0)
    def _():
        for cp in copies(step, slot):
            cp.start()

    @pl.when(step + 1 < n_steps)
    def _():
        for cp in copies(step + 1, 1 - slot):
            cp.start()

    for cp in copies(step, slot):
        cp.wait()

    query = lax.broadcasted_iota(jnp.int32, (1, dec_seq), 1)
    q_col = jnp.sum(jnp.where(query == i, q_ref[...], 0.0), axis=-1, keepdims=True) * (dh ** -0.5)
    col_f = lax.broadcasted_iota(jnp.int32, (1, MOBA_BLOCK), 1).astype(F32)
    own = lax.broadcasted_iota(jnp.int32, (1, kn_ref.shape[1]), 1)
    outs = []
    for h in range(ATT_HEADS):
        hs = slice(h * dh, (h + 1) * dh)
        qh = q_col[hs]
        slope = 2.0 ** (-(h + 1))
        so = jnp.sum(kn_ref[hs, :] * qh, axis=0, keepdims=True) + slope * own.astype(F32)
        so = jnp.where(own <= i, so, NEG_INF)
        m = jnp.max(so, axis=-1, keepdims=True)
        s_blocks = []
        for r in range(n_sel):
            rel = (picked_block(step, h, r) * MOBA_BLOCK - past_len).astype(F32)
            s = jnp.sum(kbuf[slot, h * n_sel + r] * qh, axis=0, keepdims=True) + slope * (col_f + rel)
            s_blocks.append(s)
            m = jnp.maximum(m, jnp.max(s, axis=-1, keepdims=True))
        p = jnp.exp(so - m)
        l = jnp.sum(p, axis=-1, keepdims=True)
        acc = jnp.sum(vn_ref[hs, :] * p, axis=-1, keepdims=True)
        weighted = jnp.zeros((dh, MOBA_BLOCK), F32)
        for r in range(n_sel):
            p = jnp.exp(s_blocks[r] - m)
            l = l + jnp.sum(p, axis=-1, keepdims=True)
            weighted = weighted + vbuf[slot, h * n_sel + r] * p
        outs.append((acc + jnp.sum(weighted, axis=-1, keepdims=True)) / l)
    o_col = jnp.concatenate(outs, axis=0)

    @pl.when(i == 0)
    def _():
        o_ref[...] = jnp.zeros(o_ref.shape, F32)

    o_ref[...] = jnp.where(query == i, o_col, o_ref[...])


def _moba_sample(q, k_new, v_new, cache_k, cache_v, page_table):
    db, dec_seq, _ = q.shape
    n_pages = page_table.shape[1]
    nblk = n_pages * PAGE_SIZE // MOBA_BLOCK
    n_sel = min(MOBA_TOPK, nblk)
    n_combo = ATT_HEADS * n_sel
    new_pad = 8
    pool_t = lambda c: c.transpose(0, 2, 3, 1)
    tok_t = lambda a, pad: jnp.pad(a.transpose(0, 2, 1), ((0, 0), (0, 0), (0, pad)))
    means_t = _pool_means(pool_t(cache_k), page_table, 8)
    idx = _select_blocks(q, means_t, n_sel)
    tspec = lambda n: pl.BlockSpec((None, ATT_WIDTH, n), lambda b, i, pt, ix: (b, 0, 0))
    grid_spec = pltpu.PrefetchScalarGridSpec(
        num_scalar_prefetch=2,
        grid=(db, dec_seq),
        in_specs=[tspec(dec_seq), tspec(new_pad), tspec(new_pad),
                  pl.BlockSpec(memory_space=pl.ANY), pl.BlockSpec(memory_space=pl.ANY)],
        out_specs=tspec(dec_seq),
        scratch_shapes=[pltpu.VMEM((2, n_combo, ATT_HEAD_DIM, MOBA_BLOCK), F32),
                        pltpu.VMEM((2, n_combo, ATT_HEAD_DIM, MOBA_BLOCK), F32),
                        pltpu.SemaphoreType.DMA((2,))],
    )
    out_t = pl.pallas_call(
        functools.partial(_gather_attn_kernel, n_pages=n_pages, dec_seq=dec_seq, n_sel=n_sel),
        grid_spec=grid_spec,
        out_shape=jax.ShapeDtypeStruct((db, ATT_WIDTH, dec_seq), F32),
        compiler_params=_params("arbitrary", "arbitrary"),
        name="moba_sample",
    )(page_table.reshape(-1), idx.reshape(-1), tok_t(q, 0), tok_t(k_new, new_pad - dec_seq),
      tok_t(v_new, new_pad - dec_seq), pool_t(cache_k), pool_t(cache_v))
    return out_t.transpose(0, 2, 1).reshape(db * dec_seq, ATT_WIDTH)


def _hgrn_kernel(hg_ref, lbl_ref, g_ref, s0_ref, o_ref, sout_ref,
                 st_ref, qa_ref, cum_ref, kk_ref, raw_ref, qf_ref, kf_ref, vb_ref, *, tc, n_valid):
    t = pl.program_id(1)
    w = HG_WIDTH
    c = HG_SUB
    nc = tc // c
    slab = min(tc, LANES)

    @pl.when(t == 0)
    def _():
        for h in range(HG_HEADS):
            st_ref[h] = s0_ref[h].T

    lbl = lbl_ref[...]
    e = jnp.exp(lbl - jnp.max(lbl, axis=0, keepdims=True))
    lb = e[0:1, :] / jnp.sum(e, axis=0, keepdims=True)
    f = lb + (1.0 - lb) * jax.nn.sigmoid(hg_ref[:, w:2 * w])
    lf = jnp.log(f)
    kk = 1.0 - f
    if n_valid < tc:
        valid = lax.broadcasted_iota(jnp.int32, (tc, 1), 0) < n_valid
        lf = jnp.where(valid, lf, 0.0)
        kk = jnp.where(valid, kk, 0.0)
    row = lax.broadcasted_iota(jnp.int32, (tc, 1), 0)
    pos = row % c
    cum = lf
    shift = 1
    while shift < c:
        cum = cum + jnp.where(pos >= shift, pltpu.roll(cum, shift, axis=0), 0.0)
        shift *= 2
    q = _silu(hg_ref[:, 0:w])
    last = jnp.broadcast_to(cum.reshape(nc, c, w)[:, c - 1:c, :], (nc, c, w)).reshape(tc, w)
    qa_ref[...] = q
    kk_ref[...] = kk
    cum_ref[...] = cum
    qf_ref[...] = (q * jnp.exp(cum)).astype(BF16)
    kf_ref[...] = (kk * jnp.exp(last - cum)).astype(BF16)
    vb_ref[...] = hg_ref[:, 2 * w:3 * w].astype(BF16)

    chunk_col = (row % slab) // c * c
    lane = lax.broadcasted_iota(jnp.int32, (1, slab), 1)
    sub = lax.broadcasted_iota(jnp.int32, (1, c, 1), 1)
    for h in range(HG_HEADS):
        cs = slice(h * HG_DK, (h + 1) * HG_DK)
        q3 = qa_ref[:, cs].reshape(nc, c, HG_DK)
        k3 = kk_ref[:, cs].reshape(nc, c, HG_DK)
        cum3 = cum_ref[:, cs].reshape(nc, c, HG_DK)
        attn = jnp.zeros((tc, slab), F32)
        for s in range(c):
            d = jnp.where(sub >= s, cum3 - cum3[:, s:s + 1, :], NEG_INF)
            a_col = jnp.sum(q3 * k3[:, s:s + 1, :] * jnp.exp(d), axis=-1, keepdims=True)
            attn = jnp.where(lane == chunk_col + s, a_col.reshape(tc, 1), attn)
        for g in range(tc // slab):
            rows = slice(g * slab, (g + 1) * slab)
            raw_ref[rows, cs] = _dot(attn[rows].astype(BF16), vb_ref[rows, cs])

    def chunk(ci, carry):
        r0 = pl.multiple_of(ci * c, c)
        for h in range(HG_HEADS):
            cs = slice(h * HG_DK, (h + 1) * HG_DK)
            s_t = st_ref[h]
            decay = jnp.exp(cum_ref[pl.ds(r0, c), cs][c - 1:c, :])
            raw_ref[pl.ds(r0, c), cs] += _dot_nt(qf_ref[pl.ds(r0, c), cs], s_t.astype(BF16))
            st_ref[h] = s_t * decay + _dot_tn(vb_ref[pl.ds(r0, c), cs], kf_ref[pl.ds(r0, c), cs])
        return carry

    lax.fori_loop(0, nc, chunk, 0)

    for h in range(HG_HEADS):
        cs = slice(h * HG_DV, (h + 1) * HG_DV)
        gate = _silu(hg_ref[:, 3 * w + h * HG_DV:3 * w + (h + 1) * HG_DV])
        o_ref[:, cs] = _rms(raw_ref[:, cs]) * g_ref[:, cs] * gate

    @pl.when(t == pl.num_programs(1) - 1)
    def _():
        for h in range(HG_HEADS):
            sout_ref[h] = st_ref[h].T


def _hgrn(hg, lb_logits, g, s0, batch, tc, n_valid):
    n = hg.shape[0]
    n_t = n // (batch * tc)
    state_spec = pl.BlockSpec((None, HG_HEADS, HG_DK, HG_DV), lambda b, t: (b, 0, 0, 0))
    return pl.pallas_call(
        functools.partial(_hgrn_kernel, tc=tc, n_valid=n_valid),
        grid=(batch, n_t),
        in_specs=[pl.BlockSpec((tc, hg.shape[1]), lambda b, t: (b * n_t + t, 0)),
                  pl.BlockSpec(lb_logits.shape, lambda b, t: (0, 0)),
                  pl.BlockSpec(g.shape, lambda b, t: (0, 0)),
                  state_spec],
        out_specs=[pl.BlockSpec((tc, HG_WIDTH), lambda b, t: (b * n_t + t, 0)), state_spec],
        out_shape=[jax.ShapeDtypeStruct((n, HG_WIDTH), F32),
                   jax.ShapeDtypeStruct((batch, HG_HEADS, HG_DK, HG_DV), F32)],
        scratch_shapes=[pltpu.VMEM((HG_HEADS, HG_DV, HG_DK), F32)]
        + [pltpu.VMEM((tc, HG_WIDTH), F32)] * 4 + [pltpu.VMEM((tc, HG_WIDTH), BF16)] * 3,
        compiler_params=_params("parallel", "arbitrary"),
        name="hgrn2",
    )(hg, lb_logits, g, s0)


def _route(h2, rwt, rb):
    tm = h2.shape[0]
    ng, gsz = N_EXPERT_GROUPS, GROUP_SIZE
    scores = jax.nn.sigmoid(_dot_nt(rwt, h2, precision=HI))
    s3 = scores.reshape(ng, gsz, tm)
    b3 = (scores + rb).reshape(ng, gsz, tm)
    sub = lax.broadcasted_iota(jnp.int32, (1, gsz, 1), 1).astype(F32)
    grp = lax.broadcasted_iota(jnp.int32, (ng, 1, 1), 0).astype(F32)
    m1 = jnp.max(b3, axis=1, keepdims=True)
    i1 = jnp.min(jnp.where(b3 == m1, sub, float(gsz)), axis=1, keepdims=True)
    m2 = jnp.max(jnp.where(sub == i1, TAKEN, b3), axis=1, keepdims=True)
    gs = m1 + m2
    beaten = jnp.zeros(gs.shape, F32)
    for g in range(ng):
        other = gs[g:g + 1]
        wins = jnp.logical_or(other > gs, jnp.logical_and(other == gs, grp > float(g)))
        beaten = beaten + wins.astype(F32)
    keep = beaten < float(TOPK_GROUPS)
    cand = jnp.where(keep, b3, NEG_INF)
    eidx = grp * float(gsz) + sub
    sel = jnp.zeros(cand.shape, jnp.bool_)
    for _ in range(EXPERT_TOPK):
        m = jnp.max(jnp.max(cand, axis=1, keepdims=True), axis=0, keepdims=True)
        hit_idx = jnp.where(cand == m, eidx, float(N_EXPERTS))
        idx = jnp.min(jnp.min(hit_idx, axis=1, keepdims=True), axis=0, keepdims=True)
        hit = eidx == idx
        sel = jnp.logical_or(sel, hit)
        cand = jnp.where(hit, TAKEN, cand)
    wsel = jnp.where(sel, s3, 0.0)
    tot = jnp.sum(jnp.sum(wsel, axis=1, keepdims=True), axis=0, keepdims=True)
    return (wsel / tot * ROUTED_SCALE).reshape(N_EXPERTS, tm)


def _outproj_kernel(x_ref, oa_ref, oh_ref, ag_ref, wo_ref, g1_ref, sh2_ref, sc2_ref, n2_ref,
                    rwt_ref, rb_ref, x1_ref, h2_ref, gates_ref):
    oa = (_rms(oa_ref[...]) * ag_ref[...]).astype(BF16)
    y = _dot(oa, wo_ref[0:ATT_WIDTH, :]) + _dot(oh_ref[...].astype(BF16), wo_ref[ATT_WIDTH:, :])
    x1 = x_ref[...] + g1_ref[...] * y
    x1_ref[...] = x1
    h2 = _rms(x1) * n2_ref[...]
    h2 = h2 * (1.0 + sc2_ref[...]) + sh2_ref[...]
    h2_ref[...] = h2.astype(BF16)
    gates_t = _route(h2, rwt_ref[...], rb_ref[...])
    pad = jnp.zeros((LANES - N_EXPERTS, gates_t.shape[1]), F32)
    gates_ref[...] = jnp.concatenate([gates_t, pad], axis=0).T


def _outproj(x, oa, oh, ag, wo_bf, g1, sh2, sc2, n2, rwt, rb, tm, tiles_per_batch):
    n, d = x.shape
    row = lambda c: pl.BlockSpec((tm, c), lambda i: (i, 0))
    mod = lambda a: _mod_spec(a, tm, tiles_per_batch, 1)
    return pl.pallas_call(
        _outproj_kernel,
        grid=(n // tm,),
        in_specs=[row(d), row(ATT_WIDTH), row(HG_WIDTH), _full_spec(ag, 1), _full_spec(wo_bf, 1),
                  mod(g1), mod(sh2), mod(sc2), _full_spec(n2, 1), _full_spec(rwt, 1), _full_spec(rb, 1)],
        out_specs=[row(d), row(d), row(LANES)],
        out_shape=[jax.ShapeDtypeStruct((n, d), F32), jax.ShapeDtypeStruct((n, d), BF16),
                   jax.ShapeDtypeStruct((n, LANES), F32)],
        compiler_params=_params("parallel"),
        name="out_proj_route",
    )(x, oa, oh, ag, wo_bf, g1, sh2, sc2, n2, rwt, rb)


def _moe_kernel(h2_ref, gates_ref, x1_ref, g2_ref, shf_ref, scf_ref, fg_ref,
                w1_ref, w3_ref, w2_ref, sw1_ref, sw3_ref, sw2_ref, y_ref, *, sub):
    e = pl.program_id(1)
    tm = h2_ref.shape[0]
    lane = lax.broadcasted_iota(jnp.int32, (1, LANES), 1)

    @pl.when(e == 0)
    def _():
        for r in range(0, tm, sub):
            h = h2_ref[r:r + sub, :]
            act = _silu(_dot(h, sw1_ref[...])) * _dot(h, sw3_ref[...])
            y_ref[r:r + sub, :] = _dot(act.astype(BF16), sw2_ref[...])

    for r in range(0, tm, sub):
        h = h2_ref[r:r + sub, :]
        gate = jnp.sum(jnp.where(lane == e, gates_ref[r:r + sub, :], 0.0), axis=-1, keepdims=True)
        act = _silu(_dot(h, w1_ref[...])) * _dot(h, w3_ref[...]) * gate
        y_ref[r:r + sub, :] += _dot(act.astype(BF16), w2_ref[...])

    @pl.when(e == pl.num_programs(1) - 1)
    def _():
        x2 = x1_ref[...] + g2_ref[...] * y_ref[...]
        y_ref[...] = _rms(x2) * fg_ref[...] * (1.0 + scf_ref[...]) + shf_ref[...]


def _moe(h2, gates, x1, g2, shf, scf, fg, w1, w3, w2, sw1, sw3, sw2, tm, tiles_per_batch):
    n, d = x1.shape
    n_e = w1.shape[0]
    row = lambda c: pl.BlockSpec((tm, c), lambda i, e: (i, 0))
    mod = lambda a: _mod_spec(a, tm, tiles_per_batch, 2)
    wspec = lambda a: pl.BlockSpec((None,) + a.shape[1:], lambda i, e: (e, 0, 0))
    return pl.pallas_call(
        functools.partial(_moe_kernel, sub=min(tm, 256)),
        grid=(n // tm, n_e),
        in_specs=[row(d), row(LANES), row(d), mod(g2), mod(shf), mod(scf), _full_spec(fg, 2),
                  wspec(w1), wspec(w3), wspec(w2), _full_spec(sw1, 2), _full_spec(sw3, 2), _full_spec(sw2, 2)],
        out_specs=row(d),
        out_shape=jax.ShapeDtypeStruct((n, d), F32),
        compiler_params=_params("parallel", "arbitrary"),
        name="moe_final",
    )(h2, gates, x1, g2, shf, scf, fg, w1, w3, w2, sw1, sw3, sw2)


def _split_mod(mod, n_parts, per_token_repeat):
    parts = jnp.split(mod, n_parts, axis=-1)
    if per_token_repeat is None:
        return [p[:, None, :] for p in parts]
    return [jnp.repeat(p, per_token_repeat, axis=0) for p in parts]


def kernel(x_prompt, x_sample, c_prompt, c_sample, cache_k, cache_v, state_hgrn, page_table, ada_w, ada_b, norm1_g, norm2_g, w_in, att_norm_g, hg_norm_g, hg_lb_logits, w_out, router_w, router_bias, exp_w1, exp_w3, exp_w2, shared_w1, shared_w3, shared_w2, final_g, ada_final_w, ada_final_b):
    assert ada_w.shape[0] == 1, "single trunk layer"
    batch, seq, d = x_prompt.shape
    db, dec_seq, _ = x_sample.shape
    n_p, n_s = batch * seq, db * dec_seq

    c_all = jnp.concatenate([c_prompt, c_sample], axis=0)
    c_rows = -(-c_all.shape[0] // 8) * 8
    c_all = jnp.pad(c_all, ((0, c_rows - c_all.shape[0]), (0, 0)))
    mod = _ada(c_all, ada_w[0], ada_b[0], 1024)
    modf = _ada(c_all, ada_final_w, ada_final_b, 1024)
    mods_p = _split_mod(mod[:batch], 6, None) + _split_mod(modf[:batch], 2, None)
    mods_s = _split_mod(mod[batch:batch + db], 6, dec_seq) + _split_mod(modf[batch:batch + db], 2, dec_seq)

    w_in_bf = w_in[0].astype(BF16)
    w_out_bf = w_out[0].astype(BF16)
    w1, w3, w2 = exp_w1[0].astype(BF16), exp_w3[0].astype(BF16), exp_w2[0].astype(BF16)
    sw1, sw3, sw2 = shared_w1[0].astype(BF16), shared_w3[0].astype(BF16), shared_w2[0].astype(BF16)
    n1, n2 = norm1_g[0].reshape(1, d), norm2_g[0].reshape(1, d)
    ag, hgg = att_norm_g[0].reshape(1, ATT_WIDTH), hg_norm_g[0].reshape(1, HG_WIDTH)
    fg = final_g.reshape(1, d)
    rwt = router_w[0].T
    rb = router_bias[0].reshape(N_EXPERTS, 1)
    lbl = hg_lb_logits.astype(F32)

    def tail(x2d, oa, oh, mods, tm, tiles_per_batch, tm_moe, tiles_per_batch_moe):
        sh1, sc1, g1, sh2, sc2, g2, shf, scf = mods
        x1, h2, gates = _outproj(x2d, oa, oh, ag, w_out_bf, g1, sh2, sc2, n2, rwt, rb, tm, tiles_per_batch)
        return _moe(h2, gates, x1, g2, shf, scf, fg, w1, w3, w2, sw1, sw3, sw2, tm_moe, tiles_per_batch_moe)

    xp = x_prompt.reshape(n_p, d)
    tm_p = 512
    q_p, k_p, v_p, hg_p = _inproj(xp, mods_p[0], mods_p[1], n1, w_in_bf, tm_p, seq // tm_p)
    oa_p = _moba_prompt(q_p, k_p, v_p, batch, seq)
    s0_p = jnp.zeros((batch, HG_HEADS, HG_DK, HG_DV), F32)
    oh_p, st_p = _hgrn(hg_p, lbl, hgg, s0_p, batch, 512, 512)
    y_p = tail(xp, oa_p, oh_p, mods_p, tm_p, seq // tm_p, 1024, seq // 1024)

    xs = x_sample.reshape(n_s, d)
    q_s, k_s, v_s, hg_s = _inproj(xs, mods_s[0], mods_s[1], n1, w_in_bf, n_s, 1)
    tok3 = lambda a: a.reshape(db, dec_seq, ATT_WIDTH)
    oa_s = _moba_sample(tok3(q_s), tok3(k_s), tok3(v_s), cache_k[0], cache_v[0], page_table)
    hg_s_pad = jnp.pad(hg_s.reshape(db, dec_seq, -1), ((0, 0), (0, HG_SUB - dec_seq), (0, 0)))
    oh_s, st_s = _hgrn(hg_s_pad.reshape(db * HG_SUB, -1), lbl, hgg, state_hgrn[0], db, HG_SUB, dec_seq)
    oh_s = oh_s.reshape(db, HG_SUB, HG_WIDTH)[:, :dec_seq].reshape(n_s, HG_WIDTH)
    y_s = tail(xs, oa_s, oh_s, mods_s, n_s, 1, n_s, 1)

    kv_p = (1, batch, seq, ATT_HEADS, ATT_HEAD_DIM)
    kv_s = (1, db, dec_seq, ATT_HEADS, ATT_HEAD_DIM)
    return (y_p.reshape(batch, seq, d), y_s.reshape(db, dec_seq, d),
            k_p.reshape(kv_p), v_p.reshape(kv_p), st_p[None],
            k_s.reshape(kv_s), v_s.reshape(kv_s), st_s[None])
```

```python
import functools

import jax
import jax.numpy as jnp
from jax import lax
from jax.experimental import pallas as pl
from jax.experimental.pallas import tpu as pltpu

F32 = jnp.float32
BF16 = jnp.bfloat16
HI = lax.Precision.HIGHEST

D_MODEL = 1024
PAGE_SIZE = 128
ATT_HEADS = 8
ATT_HEAD_DIM = 64
ATT_WIDTH = ATT_HEADS * ATT_HEAD_DIM
MOBA_BLOCK = 256
MOBA_TOPK = 3
HG_HEADS = 4
HG_DK = 128
HG_DV = 128
HG_WIDTH = HG_HEADS * HG_DV
HG_SUB = 16
N_EXPERTS = 64
EXPERT_TOPK = 6
N_EXPERT_GROUPS = 8
GROUP_SIZE = N_EXPERTS // N_EXPERT_GROUPS
TOPK_GROUPS = 4
EXPERT_DIM = 256
MOE_GROUP = 4
ROUTED_SCALE = 2.5
RMS_EPS = 1e-6
NEG_INF = -1e30
TAKEN = -3e38
LOG2E = 1.4426950408889634
LANES = 128
VMEM_LIMIT = 56 * 1024 * 1024


def _silu(x):
    return x * jax.nn.sigmoid(x)


def _rms(x):
    return x * lax.rsqrt(jnp.mean(x * x, axis=-1, keepdims=True) + RMS_EPS)


def _dot_nt(a, b, **kw):
    return lax.dot_general(a, b, (((1,), (1,)), ((), ())), preferred_element_type=F32, **kw)


def _dot_tn(a, b, **kw):
    return lax.dot_general(a, b, (((0,), (0,)), ((), ())), preferred_element_type=F32, **kw)


def _dot(a, b, **kw):
    return jnp.dot(a, b, preferred_element_type=F32, **kw)


def _params(*sem):
    return pltpu.CompilerParams(dimension_semantics=sem, vmem_limit_bytes=VMEM_LIMIT)


def _mod_spec(arr, tm, tiles_per_batch, grid_rank):
    if arr.ndim == 3:
        if grid_rank == 1:
            return pl.BlockSpec((None, 1, arr.shape[-1]), lambda i: (i // tiles_per_batch, 0, 0))
        return pl.BlockSpec((None, 1, arr.shape[-1]), lambda i, e: (i // tiles_per_batch, 0, 0))
    if grid_rank == 1:
        return pl.BlockSpec((tm, arr.shape[-1]), lambda i: (i, 0))
    return pl.BlockSpec((tm, arr.shape[-1]), lambda i, e: (i, 0))


def _full_spec(arr, grid_rank):
    zeros = (0,) * arr.ndim
    if grid_rank == 1:
        return pl.BlockSpec(arr.shape, lambda i: zeros)
    return pl.BlockSpec(arr.shape, lambda i, e: zeros)


def _ada_kernel(c_ref, w_ref, b_ref, o_ref):
    o_ref[...] = _dot(_silu(c_ref[...]), w_ref[...], precision=HI) + b_ref[...]


def _ada(c, w, b, tn):
    m, d = c.shape
    n = w.shape[1]
    return pl.pallas_call(
        _ada_kernel,
        grid=(n // tn,),
        in_specs=[pl.BlockSpec((m, d), lambda j: (0, 0)),
                  pl.BlockSpec((d, tn), lambda j: (0, j)),
                  pl.BlockSpec((1, tn), lambda j: (0, j))],
        out_specs=pl.BlockSpec((m, tn), lambda j: (0, j)),
        out_shape=jax.ShapeDtypeStruct((m, n), F32),
        compiler_params=_params("parallel"),
        name="ada_mod",
    )(c, w, b.reshape(1, n))


def _inproj_kernel(x_ref, sh_ref, sc_ref, g_ref, w_ref, wkvt_ref, q_ref, kt_ref, vt_ref, hg_ref):
    h = _rms(x_ref[...]) * g_ref[...]
    h = (h * (1.0 + sc_ref[...]) + sh_ref[...]).astype(BF16)
    a = ATT_WIDTH
    q_ref[...] = _dot(h, w_ref[:, 0:a])
    kt_ref[...] = _dot_nt(wkvt_ref[0:a, :], h)
    vt_ref[...] = _dot_nt(wkvt_ref[a:2 * a, :], h)
    hg_ref[...] = _dot(h, w_ref[:, a:])


def _inproj(x, sh, sc, g, w_qhg, w_kvt, tm, tiles_per_batch):
    n, d = x.shape
    hg_cols = w_qhg.shape[1] - ATT_WIDTH
    n_batches = n // (tm * tiles_per_batch)
    row = lambda c: pl.BlockSpec((tm, c), lambda i: (i, 0))
    tspec = pl.BlockSpec((None, ATT_WIDTH, tm), lambda i: (i // tiles_per_batch, 0, i % tiles_per_batch))
    t_shape = jax.ShapeDtypeStruct((n_batches, ATT_WIDTH, tm * tiles_per_batch), F32)
    return pl.pallas_call(
        _inproj_kernel,
        grid=(n // tm,),
        in_specs=[row(d), _mod_spec(sh, tm, tiles_per_batch, 1), _mod_spec(sc, tm, tiles_per_batch, 1),
                  _full_spec(g, 1), _full_spec(w_qhg, 1), _full_spec(w_kvt, 1)],
        out_specs=[row(ATT_WIDTH), tspec, tspec, row(hg_cols)],
        out_shape=[jax.ShapeDtypeStruct((n, ATT_WIDTH), F32), t_shape, t_shape,
                   jax.ShapeDtypeStruct((n, hg_cols), F32)],
        compiler_params=_params("parallel"),
        name="in_proj",
    )(x, sh, sc, g, w_qhg, w_kvt)


def _head_slope(head, shape):
    out = jnp.zeros(shape, F32)
    for i in range(ATT_HEADS):
        out = jnp.where(head == i, 2.0 ** (-(i + 1)), out)
    return out


def _top_blocks(gate, blk, n_sel, axis=-1):
    nb = gate.shape[axis]
    sel = jnp.zeros(gate.shape, jnp.bool_)
    for _ in range(n_sel):
        m = jnp.max(gate, axis=axis, keepdims=True)
        idx = jnp.min(jnp.where(gate == m, blk, float(nb)), axis=axis, keepdims=True)
        hit = blk == idx
        sel = jnp.logical_or(sel, hit)
        gate = jnp.where(hit, TAKEN, gate)
    return sel


def _moba_prompt_kernel(q_ref, k_ref, v_ref, o_ref,
                        means_ref, kbf_ref, vt_ref, base_ref, qst_ref, selt_ref, t_ref, *, nb):
    hp = pl.program_id(1)
    qb = pl.program_id(2)
    bs = MOBA_BLOCK
    dh = ATT_HEAD_DIM
    lane_head = lax.broadcasted_iota(jnp.int32, (1, 2 * bs), 1) // bs
    slope2 = _head_slope(2 * hp + lane_head, (1, 2 * bs)) * LOG2E

    @pl.when(qb == 0)
    def _():
        blk_lane = lax.broadcasted_iota(jnp.int32, (1, LANES), 1)

        def stage(j, means_t):
            j0 = pl.multiple_of(j * bs, bs)
            ktj = k_ref[:, pl.ds(j0, bs)]
            kbf_ref[j] = ktj.T.astype(BF16)
            vt_ref[j] = v_ref[:, pl.ds(j0, bs)].astype(BF16)
            return jnp.where(blk_lane == j, jnp.mean(ktj, axis=-1, keepdims=True), means_t)

        means_t = lax.fori_loop(0, nb, stage, jnp.zeros((LANES, LANES), F32))
        means_ref[...] = means_t.T[:nb]
        krow_f = lax.broadcasted_iota(jnp.int32, (bs, 2 * bs), 0).astype(F32)
        base_ref[...] = slope2 * krow_f

    qt = q_ref[...].T
    sub_head = lax.broadcasted_iota(jnp.int32, (LANES, 1), 0) // dh
    q2 = jnp.concatenate([jnp.where(sub_head == 0, qt, 0.0), jnp.where(sub_head == 1, qt, 0.0)], axis=1)
    blk = lax.broadcasted_iota(jnp.int32, (nb, 1), 0).astype(F32)
    past = blk < qb.astype(F32)
    gate = jnp.where(past, _dot(means_ref[...], q2, precision=HI), NEG_INF)
    sel = jnp.logical_and(_top_blocks(gate, blk, MOBA_TOPK, axis=0), past)
    selt_ref[...] = sel.astype(F32)
    qst_ref[...] = (q2 * (dh ** -0.5 * LOG2E)).astype(BF16)

    def scores(j, slot):
        t_ref[slot] = _dot(kbf_ref[j], qst_ref[...]) + base_ref[...]

    def weighted_values(j, p):
        vtj = vt_ref[j]
        pb = p.astype(BF16)
        return _dot(vtj[:dh, :], pb[:, :bs]), _dot(vtj[dh:, :], pb[:, bs:])

    scores(qb, 0)
    kq = lax.broadcasted_iota(jnp.int32, (bs, 2 * bs), 1) % bs
    causal = lax.broadcasted_iota(jnp.int32, (bs, 2 * bs), 0) <= kq
    s = jnp.where(causal, t_ref[0], NEG_INF)
    m = jnp.max(s, axis=0, keepdims=True)
    p = jnp.exp2(s - m)
    l = jnp.sum(p, axis=0, keepdims=True)
    acc0, acc1 = weighted_values(qb, p)
    scores(0, 0)

    def attend(j, slot, carry):
        m, l, acc0, acc1 = carry
        jc = jnp.minimum(j, nb - 1)
        c = slope2 * (float(bs) * (qb - j).astype(F32))
        picked = jnp.logical_and(selt_ref[pl.ds(jc, 1), :] > 0.0, j < qb)
        m_blk = jnp.max(t_ref[slot], axis=0, keepdims=True) - c
        m_new = jnp.where(picked, jnp.maximum(m, m_blk), m)
        alpha = jnp.exp2(m - m_new)
        u = jnp.where(picked, m_new + c, -NEG_INF)
        p = jnp.exp2(t_ref[slot] - u)
        l = alpha * l + jnp.sum(p, axis=0, keepdims=True)
        pv0, pv1 = weighted_values(jc, p)
        return m_new, l, alpha[:, :bs] * acc0 + pv0, alpha[:, bs:] * acc1 + pv1

    def body(i, carry):
        j = 2 * i
        scores(jnp.minimum(j + 1, nb - 1), 1)
        carry = attend(j, 0, carry)
        scores(jnp.minimum(j + 2, nb - 1), 0)
        return attend(j + 1, 1, carry)

    m, l, acc0, acc1 = lax.fori_loop(0, (qb + 1) // 2, body, (m, l, acc0, acc1))
    o_t = jnp.concatenate([acc0 / l[:, :bs], acc1 / l[:, bs:]], axis=0)
    o_ref[...] = o_t.T


def _moba_prompt(q, k, v, batch, seq):
    nb = seq // MOBA_BLOCK
    n_hp = ATT_WIDTH // LANES
    qspec = pl.BlockSpec((MOBA_BLOCK, LANES), lambda b, hp, qb: (b * nb + qb, hp))
    assert nb <= LANES
    kvspec = pl.BlockSpec((None, LANES, seq), lambda b, hp, qb: (b, hp, 0))
    return pl.pallas_call(
        functools.partial(_moba_prompt_kernel, nb=nb),
        grid=(batch, n_hp, nb),
        in_specs=[qspec, kvspec, kvspec],
        out_specs=qspec,
        out_shape=jax.ShapeDtypeStruct(q.shape, F32),
        scratch_shapes=[pltpu.VMEM((nb, LANES), F32),
                        pltpu.VMEM((nb, MOBA_BLOCK, LANES), BF16),
                        pltpu.VMEM((nb, LANES, MOBA_BLOCK), BF16),
                        pltpu.VMEM((MOBA_BLOCK, 2 * MOBA_BLOCK), F32),
                        pltpu.VMEM((LANES, 2 * MOBA_BLOCK), BF16),
                        pltpu.VMEM((nb, 2 * MOBA_BLOCK), F32),
                        pltpu.VMEM((2, MOBA_BLOCK, 2 * MOBA_BLOCK), F32)],
        compiler_params=_params("parallel", "parallel", "arbitrary"),
        name="moba_prompt",
    )(q, k, v)


def _pool_means_kernel(pt_ref, *refs, n_blk_step):
    del pt_ref
    pages = refs[:2 * n_blk_step]
    o_ref = refs[2 * n_blk_step]
    s = pl.program_id(1)
    lane = lax.broadcasted_iota(jnp.int32, (1, o_ref.shape[1]), 1)

    @pl.when(s == 0)
    def _():
        o_ref[...] = jnp.zeros(o_ref.shape, F32)

    acc = o_ref[...]
    for i in range(n_blk_step):
        both = pages[2 * i][...] + pages[2 * i + 1][...]
        col = jnp.sum(both, axis=-1, keepdims=True).reshape(ATT_WIDTH, 1) * (1.0 / MOBA_BLOCK)
        acc = jnp.where(lane == s * n_blk_step + i, col, acc)
    o_ref[...] = acc


def _pool_means(pool_kt, page_table, blocks_per_step):
    db, n_pages = page_table.shape
    nblk = n_pages * PAGE_SIZE // MOBA_BLOCK
    pps = 2 * blocks_per_step

    def page_spec(i):
        return pl.BlockSpec((None, ATT_HEADS, ATT_HEAD_DIM, PAGE_SIZE),
                            lambda b, s, pt: (pt[b * n_pages + s * pps + i], 0, 0, 0))

    grid_spec = pltpu.PrefetchScalarGridSpec(
        num_scalar_prefetch=1,
        grid=(db, n_pages // pps),
        in_specs=[page_spec(i) for i in range(pps)],
        out_specs=pl.BlockSpec((None, ATT_WIDTH, nblk), lambda b, s, pt: (b, 0, 0)),
    )
    return pl.pallas_call(
        functools.partial(_pool_means_kernel, n_blk_step=blocks_per_step),
        grid_spec=grid_spec,
        out_shape=jax.ShapeDtypeStruct((db, ATT_WIDTH, nblk), F32),
        compiler_params=_params("parallel", "arbitrary"),
        name="pool_means",
    )(page_table.reshape(-1), *([pool_kt] * pps))


def _select_kernel(q_ref, mt_ref, idx_ref, *, n_sel):
    q = q_ref[...]
    dec_seq = q.shape[0]
    nrow = dec_seq * ATT_HEADS
    qrep = jnp.broadcast_to(q[:, None, :], (dec_seq, ATT_HEADS, ATT_WIDTH)).reshape(nrow, ATT_WIDTH)
    row_head = lax.broadcasted_iota(jnp.int32, (nrow, 1), 0) % ATT_HEADS
    lane_head = lax.broadcasted_iota(jnp.int32, (1, ATT_WIDTH), 1) // ATT_HEAD_DIM
    gate = _dot(jnp.where(row_head == lane_head, qrep, 0.0), mt_ref[...], precision=HI)
    nblk = gate.shape[1]
    blk = lax.broadcasted_iota(jnp.int32, (1, nblk), 1).astype(F32)
    pick = lax.broadcasted_iota(jnp.int32, (1, n_sel), 1)
    out = jnp.zeros((nrow, n_sel), F32)
    for r in range(n_sel):
        m = jnp.max(gate, axis=-1, keepdims=True)
        idx = jnp.min(jnp.where(gate == m, blk, float(nblk)), axis=-1, keepdims=True)
        out = jnp.where(pick == r, idx, out)
        gate = jnp.where(blk == idx, TAKEN, gate)
    idx_ref[...] = out.astype(jnp.int32)


def _select_blocks(q3, means_t, n_sel):
    db, dec_seq, _ = q3.shape
    nblk = means_t.shape[2]
    nrow = dec_seq * ATT_HEADS
    return pl.pallas_call(
        functools.partial(_select_kernel, n_sel=n_sel),
        grid=(db,),
        in_specs=[pl.BlockSpec((None, dec_seq, ATT_WIDTH), lambda b: (b, 0, 0)),
                  pl.BlockSpec((None, ATT_WIDTH, nblk), lambda b: (b, 0, 0))],
        out_specs=pl.BlockSpec((None, nrow, n_sel), lambda b: (b, 0, 0)),
        out_shape=jax.ShapeDtypeStruct((db, nrow, n_sel), jnp.int32),
        compiler_params=_params("parallel"),
        name="moba_select",
    )(q3, means_t)


def _gather_attn_kernel(pt_ref, idx_ref, q_ref, kn_ref, vn_ref, pk_ref, pv_ref, o_ref, kbuf, vbuf, sem,
                        *, n_pages, dec_seq, n_sel):
    i = pl.program_id(1)
    step = pl.program_id(0) * dec_seq + i
    n_steps = pl.num_programs(0) * dec_seq
    slot = step % 2
    past_len = n_pages * PAGE_SIZE
    pages_per_block = MOBA_BLOCK // PAGE_SIZE
    dh = ATT_HEAD_DIM

    def picked_block(stp, h, r):
        return idx_ref[(stp * ATT_HEADS + h) * n_sel + r]

    def copies(stp, slt):
        bb = stp // dec_seq
        out = []
        for h in range(ATT_HEADS):
            for r in range(n_sel):
                blk = picked_block(stp, h, r)
                for pp in range(pages_per_block):
                    page = pt_ref[bb * n_pages + blk * pages_per_block + pp]
                    dst = pl.ds(pp * PAGE_SIZE, PAGE_SIZE)
                    c = h * n_sel + r
                    out.append(pltpu.make_async_copy(pk_ref.at[page, h], kbuf.at[slt, c, :, dst], sem.at[slt]))
                    out.append(pltpu.make_async_copy(pv_ref.at[page, h], vbuf.at[slt, c, :, dst], sem.at[slt]))
        return out

    @pl.when(step == 0)
    def _():
        for cp in copies(step, slot):
            cp.start()

    @pl.when(step + 1 < n_steps)
    def _():
        for cp in copies(step + 1, 1 - slot):
            cp.start()

    for cp in copies(step, slot):
        cp.wait()

    query = lax.broadcasted_iota(jnp.int32, (1, dec_seq), 1)
    q_col = jnp.sum(jnp.where(query == i, q_ref[...], 0.0), axis=-1, keepdims=True) * (dh ** -0.5)
    col_f = lax.broadcasted_iota(jnp.int32, (1, MOBA_BLOCK), 1).astype(F32)
    own = lax.broadcasted_iota(jnp.int32, (1, kn_ref.shape[1]), 1)
    outs = []
    for h in range(ATT_HEADS):
        hs = slice(h * dh, (h + 1) * dh)
        qh = q_col[hs]
        slope = 2.0 ** (-(h + 1))
        so = jnp.sum(kn_ref[hs, :] * qh, axis=0, keepdims=True) + slope * own.astype(F32)
        so = jnp.where(own <= i, so, NEG_INF)
        m = jnp.max(so, axis=-1, keepdims=True)
        s_blocks = []
        for r in range(n_sel):
            rel = (picked_block(step, h, r) * MOBA_BLOCK - past_len).astype(F32)
            s = jnp.sum(kbuf[slot, h * n_sel + r] * qh, axis=0, keepdims=True) + slope * (col_f + rel)
            s_blocks.append(s)
            m = jnp.maximum(m, jnp.max(s, axis=-1, keepdims=True))
        p = jnp.exp(so - m)
        l = jnp.sum(p, axis=-1, keepdims=True)
        acc = jnp.sum(vn_ref[hs, :] * p, axis=-1, keepdims=True)
        weighted = jnp.zeros((dh, MOBA_BLOCK), F32)
        for r in range(n_sel):
            p = jnp.exp(s_blocks[r] - m)
            l = l + jnp.sum(p, axis=-1, keepdims=True)
            weighted = weighted + vbuf[slot, h * n_sel + r] * p
        outs.append((acc + jnp.sum(weighted, axis=-1, keepdims=True)) / l)
    o_col = jnp.concatenate(outs, axis=0)

    @pl.when(i == 0)
    def _():
        o_ref[...] = jnp.zeros(o_ref.shape, F32)

    o_ref[...] = jnp.where(query == i, o_col, o_ref[...])


def _moba_sample(q, k_new, v_new, cache_k, cache_v, page_table):
    db, dec_seq, _ = q.shape
    n_pages = page_table.shape[1]
    nblk = n_pages * PAGE_SIZE // MOBA_BLOCK
    n_sel = min(MOBA_TOPK, nblk)
    n_combo = ATT_HEADS * n_sel
    new_pad = 8
    pool_t = lambda c: c.transpose(0, 2, 3, 1)
    tok_t = lambda a, pad: jnp.pad(a.transpose(0, 2, 1), ((0, 0), (0, 0), (0, pad)))
    means_t = _pool_means(pool_t(cache_k), page_table, 8)
    idx = _select_blocks(q, means_t, n_sel)
    tspec = lambda n: pl.BlockSpec((None, ATT_WIDTH, n), lambda b, i, pt, ix: (b, 0, 0))
    grid_spec = pltpu.PrefetchScalarGridSpec(
        num_scalar_prefetch=2,
        grid=(db, dec_seq),
        in_specs=[tspec(dec_seq), tspec(new_pad), tspec(new_pad),
                  pl.BlockSpec(memory_space=pl.ANY), pl.BlockSpec(memory_space=pl.ANY)],
        out_specs=tspec(dec_seq),
        scratch_shapes=[pltpu.VMEM((2, n_combo, ATT_HEAD_DIM, MOBA_BLOCK), F32),
                        pltpu.VMEM((2, n_combo, ATT_HEAD_DIM, MOBA_BLOCK), F32),
                        pltpu.SemaphoreType.DMA((2,))],
    )
    out_t = pl.pallas_call(
        functools.partial(_gather_attn_kernel, n_pages=n_pages, dec_seq=dec_seq, n_sel=n_sel),
        grid_spec=grid_spec,
        out_shape=jax.ShapeDtypeStruct((db, ATT_WIDTH, dec_seq), F32),
        compiler_params=_params("arbitrary", "arbitrary"),
        name="moba_sample",
    )(page_table.reshape(-1), idx.reshape(-1), tok_t(q, 0), tok_t(k_new, new_pad - dec_seq),
      tok_t(v_new, new_pad - dec_seq), pool_t(cache_k), pool_t(cache_v))
    return out_t.transpose(0, 2, 1).reshape(db * dec_seq, ATT_WIDTH)


def _hgrn_kernel(hg_ref, lbl_ref, g_ref, s0_ref, o_ref, sout_ref,
                 st_ref, qa_ref, cum_ref, kk_ref, raw_ref, qf_ref, kf_ref, vb_ref, *, tc, n_valid):
    t = pl.program_id(1)
    w = HG_WIDTH
    c = HG_SUB
    nc = tc // c
    slab = min(tc, LANES)

    @pl.when(t == 0)
    def _():
        for h in range(HG_HEADS):
            st_ref[h] = s0_ref[h].T

    lbl = lbl_ref[...]
    e = jnp.exp(lbl - jnp.max(lbl, axis=0, keepdims=True))
    lb = e[0:1, :] / jnp.sum(e, axis=0, keepdims=True)
    f = lb + (1.0 - lb) * jax.nn.sigmoid(hg_ref[:, w:2 * w])
    lf = jnp.log(f)
    kk = 1.0 - f
    if n_valid < tc:
        valid = lax.broadcasted_iota(jnp.int32, (tc, 1), 0) < n_valid
        lf = jnp.where(valid, lf, 0.0)
        kk = jnp.where(valid, kk, 0.0)
    row = lax.broadcasted_iota(jnp.int32, (tc, 1), 0)
    pos = row % c
    cum = lf
    shift = 1
    while shift < c:
        cum = cum + jnp.where(pos >= shift, pltpu.roll(cum, shift, axis=0), 0.0)
        shift *= 2
    q = _silu(hg_ref[:, 0:w])
    last = jnp.broadcast_to(cum.reshape(nc, c, w)[:, c - 1:c, :], (nc, c, w)).reshape(tc, w)
    qa_ref[...] = q
    kk_ref[...] = kk
    cum_ref[...] = cum
    qf_ref[...] = (q * jnp.exp(cum)).astype(BF16)
    kf_ref[...] = (kk * jnp.exp(last - cum)).astype(BF16)
    vb_ref[...] = hg_ref[:, 2 * w:3 * w].astype(BF16)

    chunk_col = (row % slab) // c * c
    lane = lax.broadcasted_iota(jnp.int32, (1, slab), 1)
    sub = lax.broadcasted_iota(jnp.int32, (1, c, 1), 1)
    for h in range(HG_HEADS):
        cs = slice(h * HG_DK, (h + 1) * HG_DK)
        q3 = qa_ref[:, cs].reshape(nc, c, HG_DK)
        k3 = kk_ref[:, cs].reshape(nc, c, HG_DK)
        cum3 = cum_ref[:, cs].reshape(nc, c, HG_DK)
        attn = jnp.zeros((tc, slab), F32)
        for s in range(c):
            d = jnp.where(sub >= s, cum3 - cum3[:, s:s + 1, :], NEG_INF)
            a_col = jnp.sum(q3 * k3[:, s:s + 1, :] * jnp.exp(d), axis=-1, keepdims=True)
            attn = jnp.where(lane == chunk_col + s, a_col.reshape(tc, 1), attn)
        for g in range(tc // slab):
            rows = slice(g * slab, (g + 1) * slab)
            raw_ref[rows, cs] = _dot(attn[rows].astype(BF16), vb_ref[rows, cs])

    def chunk(ci, carry):
        r0 = pl.multiple_of(ci * c, c)
        for h in range(HG_HEADS):
            cs = slice(h * HG_DK, (h + 1) * HG_DK)
            s_t = st_ref[h]
            decay = jnp.exp(cum_ref[pl.ds(r0, c), cs][c - 1:c, :])
            raw_ref[pl.ds(r0, c), cs] += _dot_nt(qf_ref[pl.ds(r0, c), cs], s_t.astype(BF16))
            st_ref[h] = s_t * decay + _dot_tn(vb_ref[pl.ds(r0, c), cs], kf_ref[pl.ds(r0, c), cs])
        return carry

    lax.fori_loop(0, nc, chunk, 0, unroll=min(nc, 2))

    for h in range(HG_HEADS):
        cs = slice(h * HG_DV, (h + 1) * HG_DV)
        gate = _silu(hg_ref[:, 3 * w + h * HG_DV:3 * w + (h + 1) * HG_DV])
        o_ref[:, cs] = _rms(raw_ref[:, cs]) * g_ref[:, cs] * gate

    @pl.when(t == pl.num_programs(1) - 1)
    def _():
        for h in range(HG_HEADS):
            sout_ref[h] = st_ref[h].T


def _hgrn(hg, lb_logits, g, s0, batch, tc, n_valid):
    n = hg.shape[0]
    n_t = n // (batch * tc)
    state_spec = pl.BlockSpec((None, HG_HEADS, HG_DK, HG_DV), lambda b, t: (b, 0, 0, 0))
    return pl.pallas_call(
        functools.partial(_hgrn_kernel, tc=tc, n_valid=n_valid),
        grid=(batch, n_t),
        in_specs=[pl.BlockSpec((tc, hg.shape[1]), lambda b, t: (b * n_t + t, 0)),
                  pl.BlockSpec(lb_logits.shape, lambda b, t: (0, 0)),
                  pl.BlockSpec(g.shape, lambda b, t: (0, 0)),
                  state_spec],
        out_specs=[pl.BlockSpec((tc, HG_WIDTH), lambda b, t: (b * n_t + t, 0)), state_spec],
        out_shape=[jax.ShapeDtypeStruct((n, HG_WIDTH), F32),
                   jax.ShapeDtypeStruct((batch, HG_HEADS, HG_DK, HG_DV), F32)],
        scratch_shapes=[pltpu.VMEM((HG_HEADS, HG_DV, HG_DK), F32)]
        + [pltpu.VMEM((tc, HG_WIDTH), F32)] * 4 + [pltpu.VMEM((tc, HG_WIDTH), BF16)] * 3,
        compiler_params=_params("parallel", "arbitrary"),
        name="hgrn2",
    )(hg, lb_logits, g, s0)


def _route(h2, rwt, rb):
    tm = h2.shape[0]
    ng, gsz = N_EXPERT_GROUPS, GROUP_SIZE
    scores = jax.nn.sigmoid(_dot_nt(rwt, h2, precision=HI))
    s3 = scores.reshape(ng, gsz, tm)
    b3 = (scores + rb).reshape(ng, gsz, tm)
    sub = lax.broadcasted_iota(jnp.int32, (1, gsz, 1), 1).astype(F32)
    grp = lax.broadcasted_iota(jnp.int32, (ng, 1, 1), 0).astype(F32)
    m1 = jnp.max(b3, axis=1, keepdims=True)
    i1 = jnp.min(jnp.where(b3 == m1, sub, float(gsz)), axis=1, keepdims=True)
    m2 = jnp.max(jnp.where(sub == i1, TAKEN, b3), axis=1, keepdims=True)
    gs = m1 + m2
    beaten = jnp.zeros(gs.shape, F32)
    for g in range(ng):
        other = gs[g:g + 1]
        wins = jnp.logical_or(other > gs, jnp.logical_and(other == gs, grp > float(g)))
        beaten = beaten + wins.astype(F32)
    keep = beaten < float(TOPK_GROUPS)
    cand = jnp.where(keep, b3, NEG_INF)
    eidx = grp * float(gsz) + sub
    sel = jnp.zeros(cand.shape, jnp.bool_)
    for _ in range(EXPERT_TOPK):
        m = jnp.max(jnp.max(cand, axis=1, keepdims=True), axis=0, keepdims=True)
        hit_idx = jnp.where(cand == m, eidx, float(N_EXPERTS))
        idx = jnp.min(jnp.min(hit_idx, axis=1, keepdims=True), axis=0, keepdims=True)
        hit = eidx == idx
        sel = jnp.logical_or(sel, hit)
        cand = jnp.where(hit, TAKEN, cand)
    wsel = jnp.where(sel, s3, 0.0)
    tot = jnp.sum(jnp.sum(wsel, axis=1, keepdims=True), axis=0, keepdims=True)
    return (wsel / tot * ROUTED_SCALE).reshape(N_EXPERTS, tm)


def _outproj_kernel(x_ref, oa_ref, oh_ref, ag_ref, wo_ref, g1_ref, sh2_ref, sc2_ref, n2_ref,
                    rwt_ref, rb_ref, x1_ref, h2_ref, gates_ref):
    oa = (_rms(oa_ref[...]) * ag_ref[...]).astype(BF16)
    y = _dot(oa, wo_ref[0:ATT_WIDTH, :]) + _dot(oh_ref[...].astype(BF16), wo_ref[ATT_WIDTH:, :])
    x1 = x_ref[...] + g1_ref[...] * y
    x1_ref[...] = x1
    h2 = _rms(x1) * n2_ref[...]
    h2 = h2 * (1.0 + sc2_ref[...]) + sh2_ref[...]
    h2_ref[...] = h2.astype(BF16)
    gates_t = _route(h2, rwt_ref[...], rb_ref[...])
    pad = jnp.zeros((LANES - N_EXPERTS, gates_t.shape[1]), F32)
    gates_ref[...] = jnp.concatenate([gates_t, pad], axis=0).T


def _outproj(x, oa, oh, ag, wo_bf, g1, sh2, sc2, n2, rwt, rb, tm, tiles_per_batch):
    n, d = x.shape
    row = lambda c: pl.BlockSpec((tm, c), lambda i: (i, 0))
    mod = lambda a: _mod_spec(a, tm, tiles_per_batch, 1)
    return pl.pallas_call(
        _outproj_kernel,
        grid=(n // tm,),
        in_specs=[row(d), row(ATT_WIDTH), row(HG_WIDTH), _full_spec(ag, 1), _full_spec(wo_bf, 1),
                  mod(g1), mod(sh2), mod(sc2), _full_spec(n2, 1), _full_spec(rwt, 1), _full_spec(rb, 1)],
        out_specs=[row(d), row(d), row(LANES)],
        out_shape=[jax.ShapeDtypeStruct((n, d), F32), jax.ShapeDtypeStruct((n, d), BF16),
                   jax.ShapeDtypeStruct((n, LANES), F32)],
        compiler_params=_params("parallel"),
        name="out_proj_route",
    )(x, oa, oh, ag, wo_bf, g1, sh2, sc2, n2, rwt, rb)


def _moe_kernel(h2_ref, gates_ref, x1_ref, g2_ref, shf_ref, scf_ref, fg_ref,
                w1_ref, w3_ref, w2_ref, sw1_ref, sw3_ref, sw2_ref, y_ref, *, sub):
    g = pl.program_id(1)
    tm = h2_ref.shape[0]
    lane = lax.broadcasted_iota(jnp.int32, (1, LANES), 1)

    @pl.when(g == 0)
    def _():
        for r in range(0, tm, sub):
            h = h2_ref[r:r + sub, :]
            act = _silu(_dot(h, sw1_ref[...])) * _dot(h, sw3_ref[...])
            y_ref[r:r + sub, :] = _dot(act.astype(BF16), sw2_ref[...])

    for r in range(0, tm, sub):
        h = h2_ref[r:r + sub, :]
        gates = gates_ref[r:r + sub, :]
        acts = []
        for k in range(MOE_GROUP):
            gate = jnp.sum(jnp.where(lane == g * MOE_GROUP + k, gates, 0.0), axis=-1, keepdims=True)
            act = _silu(_dot(h, w1_ref[k])) * _dot(h, w3_ref[k]) * gate
            acts.append(act.astype(BF16))
        y_ref[r:r + sub, :] += _dot(jnp.concatenate(acts, axis=-1), w2_ref[...])

    @pl.when(g == pl.num_programs(1) - 1)
    def _():
        x2 = x1_ref[...] + g2_ref[...] * y_ref[...]
        y_ref[...] = _rms(x2) * fg_ref[...] * (1.0 + scf_ref[...]) + shf_ref[...]


def _moe(h2, gates, x1, g2, shf, scf, fg, w1, w3, w2, sw1, sw3, sw2, tm, tiles_per_batch):
    n, d = x1.shape
    row = lambda c: pl.BlockSpec((tm, c), lambda i, g: (i, 0))
    mod = lambda a: _mod_spec(a, tm, tiles_per_batch, 2)
    wspec = lambda a: pl.BlockSpec((None,) + a.shape[1:], lambda i, g: (g,) + (0,) * (a.ndim - 1))
    return pl.pallas_call(
        functools.partial(_moe_kernel, sub=min(tm, 256)),
        grid=(n // tm, w1.shape[0]),
        in_specs=[row(d), row(LANES), row(d), mod(g2), mod(shf), mod(scf), _full_spec(fg, 2),
                  wspec(w1), wspec(w3), wspec(w2), _full_spec(sw1, 2), _full_spec(sw3, 2), _full_spec(sw2, 2)],
        out_specs=row(d),
        out_shape=jax.ShapeDtypeStruct((n, d), F32),
        compiler_params=_params("parallel", "arbitrary"),
        name="moe_final",
    )(h2, gates, x1, g2, shf, scf, fg, w1, w3, w2, sw1, sw3, sw2)


def _split_mod(mod, n_parts, per_token_repeat):
    parts = jnp.split(mod, n_parts, axis=-1)
    if per_token_repeat is None:
        return [p[:, None, :] for p in parts]
    return [jnp.repeat(p, per_token_repeat, axis=0) for p in parts]


def kernel(x_prompt, x_sample, c_prompt, c_sample, cache_k, cache_v, state_hgrn, page_table, ada_w, ada_b, norm1_g, norm2_g, w_in, att_norm_g, hg_norm_g, hg_lb_logits, w_out, router_w, router_bias, exp_w1, exp_w3, exp_w2, shared_w1, shared_w3, shared_w2, final_g, ada_final_w, ada_final_b):
    assert ada_w.shape[0] == 1, "single trunk layer"
    batch, seq, d = x_prompt.shape
    db, dec_seq, _ = x_sample.shape
    n_p, n_s = batch * seq, db * dec_seq

    c_all = jnp.concatenate([c_prompt, c_sample], axis=0)
    c_rows = -(-c_all.shape[0] // 8) * 8
    c_all = jnp.pad(c_all, ((0, c_rows - c_all.shape[0]), (0, 0)))
    mod = _ada(c_all, ada_w[0], ada_b[0], 1024)
    modf = _ada(c_all, ada_final_w, ada_final_b, 1024)
    mods_p = _split_mod(mod[:batch], 6, None) + _split_mod(modf[:batch], 2, None)
    mods_s = _split_mod(mod[batch:batch + db], 6, dec_seq) + _split_mod(modf[batch:batch + db], 2, dec_seq)

    a = ATT_WIDTH
    w_qhg = jnp.concatenate([w_in[0][:, :a], w_in[0][:, 3 * a:]], axis=1).astype(BF16)
    w_kvt = w_in[0][:, a:3 * a].T.astype(BF16)
    w_out_bf = w_out[0].astype(BF16)
    n_grp = N_EXPERTS // MOE_GROUP
    w1 = exp_w1[0].astype(BF16).reshape(n_grp, MOE_GROUP, d, EXPERT_DIM)
    w3 = exp_w3[0].astype(BF16).reshape(n_grp, MOE_GROUP, d, EXPERT_DIM)
    w2 = exp_w2[0].astype(BF16).reshape(n_grp, MOE_GROUP * EXPERT_DIM, d)
    sw1, sw3, sw2 = shared_w1[0].astype(BF16), shared_w3[0].astype(BF16), shared_w2[0].astype(BF16)
    n1, n2 = norm1_g[0].reshape(1, d), norm2_g[0].reshape(1, d)
    ag, hgg = att_norm_g[0].reshape(1, ATT_WIDTH), hg_norm_g[0].reshape(1, HG_WIDTH)
    fg = final_g.reshape(1, d)
    rwt = router_w[0].T
    rb = router_bias[0].reshape(N_EXPERTS, 1)
    lbl = hg_lb_logits.astype(F32)

    def tail(x2d, oa, oh, mods, tm, tiles_per_batch, tm_moe, tiles_per_batch_moe):
        sh1, sc1, g1, sh2, sc2, g2, shf, scf = mods
        x1, h2, gates = _outproj(x2d, oa, oh, ag, w_out_bf, g1, sh2, sc2, n2, rwt, rb, tm, tiles_per_batch)
        return _moe(h2, gates, x1, g2, shf, scf, fg, w1, w3, w2, sw1, sw3, sw2, tm_moe, tiles_per_batch_moe)

    xp = x_prompt.reshape(n_p, d)
    tm_p = 512
    q_p, kt_p, vt_p, hg_p = _inproj(xp, mods_p[0], mods_p[1], n1, w_qhg, w_kvt, tm_p, seq // tm_p)
    oa_p = _moba_prompt(q_p, kt_p, vt_p, batch, seq)
    s0_p = jnp.zeros((batch, HG_HEADS, HG_DK, HG_DV), F32)
    oh_p, st_p = _hgrn(hg_p, lbl, hgg, s0_p, batch, 512, 512)
    y_p = tail(xp, oa_p, oh_p, mods_p, tm_p, seq // tm_p, 1024, seq // 1024)

    xs = x_sample.reshape(n_s, d)
    q_s, kt_s, vt_s, hg_s = _inproj(xs, mods_s[0], mods_s[1], n1, w_qhg, w_kvt, n_s, 1)
    k_s, v_s = kt_s[0].T, vt_s[0].T
    tok3 = lambda a: a.reshape(db, dec_seq, ATT_WIDTH)
    oa_s = _moba_sample(tok3(q_s), tok3(k_s), tok3(v_s), cache_k[0], cache_v[0], page_table)
    hg_s_pad = jnp.pad(hg_s.reshape(db, dec_seq, -1), ((0, 0), (0, HG_SUB - dec_seq), (0, 0)))
    oh_s, st_s = _hgrn(hg_s_pad.reshape(db * HG_SUB, -1), lbl, hgg, state_hgrn[0], db, HG_SUB, dec_seq)
    oh_s = oh_s.reshape(db, HG_SUB, HG_WIDTH)[:, :dec_seq].reshape(n_s, HG_WIDTH)
    y_s = tail(xs, oa_s, oh_s, mods_s, n_s, 1, n_s, 1)

    kv_p = lambda t: t.reshape(batch, ATT_HEADS, ATT_HEAD_DIM, seq).transpose(0, 3, 1, 2)[None]
    kv_s = (1, db, dec_seq, ATT_HEADS, ATT_HEAD_DIM)
    return (y_p.reshape(batch, seq, d), y_s.reshape(db, dec_seq, d),
            kv_p(kt_p), kv_p(vt_p), st_p[None],
            k_s.reshape(kv_s), v_s.reshape(kv_s), st_s[None])
```

```python
import functools

import jax
import jax.numpy as jnp
from jax import lax
from jax.experimental import pallas as pl
from jax.experimental.pallas import tpu as pltpu

F32 = jnp.float32
BF16 = jnp.bfloat16
HI = lax.Precision.HIGHEST

D_MODEL = 1024
PAGE_SIZE = 128
ATT_HEADS = 8
ATT_HEAD_DIM = 64
ATT_WIDTH = ATT_HEADS * ATT_HEAD_DIM
MOBA_BLOCK = 256
MOBA_TOPK = 3
PV_ROWS = ATT_HEAD_DIM + 16
HG_HEADS = 4
HG_DK = 128
HG_DV = 128
HG_WIDTH = HG_HEADS * HG_DV
HG_SUB = 16
N_EXPERTS = 64
EXPERT_TOPK = 6
N_EXPERT_GROUPS = 8
GROUP_SIZE = N_EXPERTS // N_EXPERT_GROUPS
TOPK_GROUPS = 4
EXPERT_DIM = 256
MOE_GROUP = 8
ROUTED_SCALE = 2.5
RMS_EPS = 1e-6
NEG_INF = -1e30
TAKEN = -3e38
LOG2E = 1.4426950408889634
LANES = 128
VMEM_LIMIT = 56 * 1024 * 1024


def _silu(x):
    return x * jax.nn.sigmoid(x)


def _rms(x):
    return x * lax.rsqrt(jnp.mean(x * x, axis=-1, keepdims=True) + RMS_EPS)


def _dot_nt(a, b, **kw):
    return lax.dot_general(a, b, (((1,), (1,)), ((), ())), preferred_element_type=F32, **kw)


def _dot_tn(a, b, **kw):
    return lax.dot_general(a, b, (((0,), (0,)), ((), ())), preferred_element_type=F32, **kw)


def _dot(a, b, **kw):
    return jnp.dot(a, b, preferred_element_type=F32, **kw)


def _params(*sem):
    return pltpu.CompilerParams(dimension_semantics=sem, vmem_limit_bytes=VMEM_LIMIT)


def _mod_spec(arr, tm, tiles_per_batch, grid_rank):
    if arr.ndim == 3:
        if grid_rank == 1:
            return pl.BlockSpec((None, 1, arr.shape[-1]), lambda i: (i // tiles_per_batch, 0, 0))
        return pl.BlockSpec((None, 1, arr.shape[-1]), lambda i, e: (i // tiles_per_batch, 0, 0))
    if grid_rank == 1:
        return pl.BlockSpec((tm, arr.shape[-1]), lambda i: (i, 0))
    return pl.BlockSpec((tm, arr.shape[-1]), lambda i, e: (i, 0))


def _full_spec(arr, grid_rank):
    zeros = (0,) * arr.ndim
    if grid_rank == 1:
        return pl.BlockSpec(arr.shape, lambda i: zeros)
    return pl.BlockSpec(arr.shape, lambda i, e: zeros)


def _ada_kernel(c_ref, w_ref, b_ref, o_ref):
    o_ref[...] = _dot(_silu(c_ref[...]), w_ref[...], precision=HI) + b_ref[...]


def _ada(c, w, b, tn):
    m, d = c.shape
    n = w.shape[1]
    return pl.pallas_call(
        _ada_kernel,
        grid=(n // tn,),
        in_specs=[pl.BlockSpec((m, d), lambda j: (0, 0)),
                  pl.BlockSpec((d, tn), lambda j: (0, j)),
                  pl.BlockSpec((1, tn), lambda j: (0, j))],
        out_specs=pl.BlockSpec((m, tn), lambda j: (0, j)),
        out_shape=jax.ShapeDtypeStruct((m, n), F32),
        compiler_params=_params("parallel"),
        name="ada_mod",
    )(c, w, b.reshape(1, n))


def _inproj_kernel(x_ref, sh_ref, sc_ref, g_ref, w_ref, wkvt_ref, q_ref, kt_ref, vt_ref, hg_ref):
    h = _rms(x_ref[...]) * g_ref[...]
    h = (h * (1.0 + sc_ref[...]) + sh_ref[...]).astype(BF16)
    a = ATT_WIDTH
    q_ref[...] = _dot(h, w_ref[:, 0:a])
    kt_ref[...] = _dot_nt(wkvt_ref[0:a, :], h)
    vt_ref[...] = _dot_nt(wkvt_ref[a:2 * a, :], h)
    hg_ref[...] = _dot(h, w_ref[:, a:])


def _inproj(x, sh, sc, g, w_qhg, w_kvt, tm, tiles_per_batch):
    n, d = x.shape
    hg_cols = w_qhg.shape[1] - ATT_WIDTH
    n_batches = n // (tm * tiles_per_batch)
    row = lambda c: pl.BlockSpec((tm, c), lambda i: (i, 0))
    tspec = pl.BlockSpec((None, ATT_WIDTH, tm), lambda i: (i // tiles_per_batch, 0, i % tiles_per_batch))
    t_shape = jax.ShapeDtypeStruct((n_batches, ATT_WIDTH, tm * tiles_per_batch), F32)
    return pl.pallas_call(
        _inproj_kernel,
        grid=(n // tm,),
        in_specs=[row(d), _mod_spec(sh, tm, tiles_per_batch, 1), _mod_spec(sc, tm, tiles_per_batch, 1),
                  _full_spec(g, 1), _full_spec(w_qhg, 1), _full_spec(w_kvt, 1)],
        out_specs=[row(ATT_WIDTH), tspec, tspec, row(hg_cols)],
        out_shape=[jax.ShapeDtypeStruct((n, ATT_WIDTH), F32), t_shape, t_shape,
                   jax.ShapeDtypeStruct((n, hg_cols), F32)],
        compiler_params=_params("parallel"),
        name="in_proj",
    )(x, sh, sc, g, w_qhg, w_kvt)


def _head_slope(head, shape):
    out = jnp.zeros(shape, F32)
    for i in range(ATT_HEADS):
        out = jnp.where(head == i, 2.0 ** (-(i + 1)), out)
    return out


def _top_blocks(gate, blk, n_sel, axis=-1):
    nb = gate.shape[axis]
    sel = jnp.zeros(gate.shape, jnp.bool_)
    for _ in range(n_sel):
        m = jnp.max(gate, axis=axis, keepdims=True)
        idx = jnp.min(jnp.where(gate == m, blk, float(nb)), axis=axis, keepdims=True)
        hit = blk == idx
        sel = jnp.logical_or(sel, hit)
        gate = jnp.where(hit, TAKEN, gate)
    return sel


def _moba_prompt_kernel(q_ref, k_ref, v_ref, o_ref,
                        means_ref, kbf_ref, vt_ref, base_ref, qst_ref, selt_ref, t_ref, *, nb):
    hp = pl.program_id(1)
    qb = pl.program_id(2)
    bs = MOBA_BLOCK
    dh = ATT_HEAD_DIM
    lane_head = lax.broadcasted_iota(jnp.int32, (1, 2 * bs), 1) // bs
    slope2 = _head_slope(2 * hp + lane_head, (1, 2 * bs)) * LOG2E

    @pl.when(qb == 0)
    def _():
        blk_lane = lax.broadcasted_iota(jnp.int32, (1, LANES), 1)
        ones_rows = (lax.broadcasted_iota(jnp.int32, (PV_ROWS - dh, bs), 0) == 0).astype(F32)

        def stage(j, means_t):
            j0 = pl.multiple_of(j * bs, bs)
            ktj = k_ref[:, pl.ds(j0, bs)]
            kbf_ref[j] = ktj.T.astype(BF16)
            vtj = v_ref[:, pl.ds(j0, bs)]
            vt_ref[j] = jnp.concatenate([vtj[:dh], ones_rows, vtj[dh:], ones_rows], axis=0).astype(BF16)
            return jnp.where(blk_lane == j, jnp.mean(ktj, axis=-1, keepdims=True), means_t)

        means_t = lax.fori_loop(0, nb, stage, jnp.zeros((LANES, LANES), F32))
        means_ref[...] = means_t.T[:nb]
        krow_f = lax.broadcasted_iota(jnp.int32, (bs, 2 * bs), 0).astype(F32)
        base_ref[...] = slope2 * krow_f

    qt = q_ref[...].T
    sub_head = lax.broadcasted_iota(jnp.int32, (LANES, 1), 0) // dh
    q2 = jnp.concatenate([jnp.where(sub_head == 0, qt, 0.0), jnp.where(sub_head == 1, qt, 0.0)], axis=1)
    blk = lax.broadcasted_iota(jnp.int32, (nb, 1), 0).astype(F32)
    past = blk < qb.astype(F32)
    gate = jnp.where(past, _dot(means_ref[...], q2, precision=HI), NEG_INF)
    sel = jnp.logical_and(_top_blocks(gate, blk, MOBA_TOPK, axis=0), past)
    selt_ref[...] = sel.astype(F32)
    qst_ref[...] = (q2 * (dh ** -0.5 * LOG2E)).astype(BF16)

    def scores(j, slot):
        t_ref[slot] = _dot(kbf_ref[j], qst_ref[...]) + base_ref[...]

    def weighted_values(j, p):
        vtj = vt_ref[j]
        pb = p.astype(BF16)
        return _dot(vtj[:PV_ROWS, :], pb[:, :bs]), _dot(vtj[PV_ROWS:, :], pb[:, bs:])

    scores(qb, 0)
    kq = lax.broadcasted_iota(jnp.int32, (bs, 2 * bs), 1) % bs
    causal = lax.broadcasted_iota(jnp.int32, (bs, 2 * bs), 0) <= kq
    s = jnp.where(causal, t_ref[0], NEG_INF)
    m = jnp.max(s, axis=0, keepdims=True)
    acc0, acc1 = weighted_values(qb, jnp.exp2(s - m))
    scores(0, 0)

    def attend(j, slot, carry):
        m, acc0, acc1 = carry
        jc = jnp.minimum(j, nb - 1)
        c = slope2 * (float(bs) * (qb - j).astype(F32))
        picked = jnp.logical_and(selt_ref[pl.ds(jc, 1), :] > 0.0, j < qb)
        m_blk = jnp.max(t_ref[slot], axis=0, keepdims=True) - c
        m_new = jnp.where(picked, jnp.maximum(m, m_blk), m)
        alpha = jnp.exp2(m - m_new)
        u = jnp.where(picked, m_new + c, -NEG_INF)
        pv0, pv1 = weighted_values(jc, jnp.exp2(t_ref[slot] - u))
        return m_new, alpha[:, :bs] * acc0 + pv0, alpha[:, bs:] * acc1 + pv1

    scores(jnp.minimum(1, nb - 1), 1)

    def body(i, carry):
        j = 2 * i
        carry = attend(j, 0, carry)
        scores(jnp.minimum(j + 2, nb - 1), 0)
        carry = attend(j + 1, 1, carry)
        scores(jnp.minimum(j + 3, nb - 1), 1)
        return carry

    m, acc0, acc1 = lax.fori_loop(0, (qb + 1) // 2, body, (m, acc0, acc1))
    o_t = jnp.concatenate([acc0[:dh] / acc0[dh:dh + 1], acc1[:dh] / acc1[dh:dh + 1]], axis=0)
    o_ref[...] = o_t.T


def _moba_prompt(q, k, v, batch, seq):
    nb = seq // MOBA_BLOCK
    n_hp = ATT_WIDTH // LANES
    qspec = pl.BlockSpec((MOBA_BLOCK, LANES), lambda b, hp, qb: (b * nb + qb, hp))
    assert nb <= LANES
    kvspec = pl.BlockSpec((None, LANES, seq), lambda b, hp, qb: (b, hp, 0))
    return pl.pallas_call(
        functools.partial(_moba_prompt_kernel, nb=nb),
        grid=(batch, n_hp, nb),
        in_specs=[qspec, kvspec, kvspec],
        out_specs=qspec,
        out_shape=jax.ShapeDtypeStruct(q.shape, F32),
        scratch_shapes=[pltpu.VMEM((nb, LANES), F32),
                        pltpu.VMEM((nb, MOBA_BLOCK, LANES), BF16),
                        pltpu.VMEM((nb, 2 * PV_ROWS, MOBA_BLOCK), BF16),
                        pltpu.VMEM((MOBA_BLOCK, 2 * MOBA_BLOCK), F32),
                        pltpu.VMEM((LANES, 2 * MOBA_BLOCK), BF16),
                        pltpu.VMEM((nb, 2 * MOBA_BLOCK), F32),
                        pltpu.VMEM((2, MOBA_BLOCK, 2 * MOBA_BLOCK), F32)],
        compiler_params=_params("parallel", "parallel", "arbitrary"),
        name="moba_prompt",
    )(q, k, v)


def _pool_means_kernel(pt_ref, *refs, n_blk_step):
    del pt_ref
    pages = refs[:2 * n_blk_step]
    o_ref = refs[2 * n_blk_step]
    s = pl.program_id(1)
    lane = lax.broadcasted_iota(jnp.int32, (1, o_ref.shape[1]), 1)

    @pl.when(s == 0)
    def _():
        o_ref[...] = jnp.zeros(o_ref.shape, F32)

    acc = o_ref[...]
    for i in range(n_blk_step):
        both = pages[2 * i][...] + pages[2 * i + 1][...]
        col = jnp.sum(both, axis=-1, keepdims=True).reshape(ATT_WIDTH, 1) * (1.0 / MOBA_BLOCK)
        acc = jnp.where(lane == s * n_blk_step + i, col, acc)
    o_ref[...] = acc


def _pool_means(pool_kt, page_table, blocks_per_step):
    db, n_pages = page_table.shape
    nblk = n_pages * PAGE_SIZE // MOBA_BLOCK
    pps = 2 * blocks_per_step

    def page_spec(i):
        return pl.BlockSpec((None, ATT_HEADS, ATT_HEAD_DIM, PAGE_SIZE),
                            lambda b, s, pt: (pt[b * n_pages + s * pps + i], 0, 0, 0))

    grid_spec = pltpu.PrefetchScalarGridSpec(
        num_scalar_prefetch=1,
        grid=(db, n_pages // pps),
        in_specs=[page_spec(i) for i in range(pps)],
        out_specs=pl.BlockSpec((None, ATT_WIDTH, nblk), lambda b, s, pt: (b, 0, 0)),
    )
    return pl.pallas_call(
        functools.partial(_pool_means_kernel, n_blk_step=blocks_per_step),
        grid_spec=grid_spec,
        out_shape=jax.ShapeDtypeStruct((db, ATT_WIDTH, nblk), F32),
        compiler_params=_params("parallel", "arbitrary"),
        name="pool_means",
    )(page_table.reshape(-1), *([pool_kt] * pps))


def _select_kernel(q_ref, mt_ref, idx_ref, *, n_sel):
    q = q_ref[...]
    dec_seq = q.shape[0]
    nrow = dec_seq * ATT_HEADS
    qrep = jnp.broadcast_to(q[:, None, :], (dec_seq, ATT_HEADS, ATT_WIDTH)).reshape(nrow, ATT_WIDTH)
    row_head = lax.broadcasted_iota(jnp.int32, (nrow, 1), 0) % ATT_HEADS
    lane_head = lax.broadcasted_iota(jnp.int32, (1, ATT_WIDTH), 1) // ATT_HEAD_DIM
    gate = _dot(jnp.where(row_head == lane_head, qrep, 0.0), mt_ref[...], precision=HI)
    nblk = gate.shape[1]
    blk = lax.broadcasted_iota(jnp.int32, (1, nblk), 1).astype(F32)
    pick = lax.broadcasted_iota(jnp.int32, (1, n_sel), 1)
    out = jnp.zeros((nrow, n_sel), F32)
    for r in range(n_sel):
        m = jnp.max(gate, axis=-1, keepdims=True)
        idx = jnp.min(jnp.where(gate == m, blk, float(nblk)), axis=-1, keepdims=True)
        out = jnp.where(pick == r, idx, out)
        gate = jnp.where(blk == idx, TAKEN, gate)
    idx_ref[...] = out.astype(jnp.int32)


def _select_blocks(q3, means_t, n_sel):
    db, dec_seq, _ = q3.shape
    nblk = means_t.shape[2]
    nrow = dec_seq * ATT_HEADS
    return pl.pallas_call(
        functools.partial(_select_kernel, n_sel=n_sel),
        grid=(db,),
        in_specs=[pl.BlockSpec((None, dec_seq, ATT_WIDTH), lambda b: (b, 0, 0)),
                  pl.BlockSpec((None, ATT_WIDTH, nblk), lambda b: (b, 0, 0))],
        out_specs=pl.BlockSpec((None, nrow, n_sel), lambda b: (b, 0, 0)),
        out_shape=jax.ShapeDtypeStruct((db, nrow, n_sel), jnp.int32),
        compiler_params=_params("parallel"),
        name="moba_select",
    )(q3, means_t)


def _gather_attn_kernel(pt_ref, idx_ref, q_ref, kn_ref, vn_ref, pk_ref, pv_ref, o_ref, kbuf, vbuf, sem,
                        *, n_pages, dec_seq, n_sel):
    i = pl.program_id(1)
    step = pl.program_id(0) * dec_seq + i
    n_steps = pl.num_programs(0) * dec_seq
    slot = step % 2
    past_len = n_pages * PAGE_SIZE
    pages_per_block = MOBA_BLOCK // PAGE_SIZE
    dh = ATT_HEAD_DIM

    def picked_block(stp, h, r):
        return idx_ref[(stp * ATT_HEADS + h) * n_sel + r]

    def copies(stp, slt):
        bb = stp // dec_seq
        out = []
        for h in range(ATT_HEADS):
            for r in range(n_sel):
                blk = picked_block(stp, h, r)
                for pp in range(pages_per_block):
                    page = pt_ref[bb * n_pages + blk * pages_per_block + pp]
                    dst = pl.ds(pp * PAGE_SIZE, PAGE_SIZE)
                    c = h * n_sel + r
                    out.append(pltpu.make_async_copy(pk_ref.at[page, h], kbuf.at[slt, c, :, dst], sem.at[slt]))
                    out.append(pltpu.make_async_copy(pv_ref.at[page, h], vbuf.at[slt, c, :, dst], sem.at[slt]))
        return out

    @pl.when(step == 0)
    def _():
        for cp in copies(step, slot):
            cp.start()

    @pl.when(step + 1 < n_steps)
    def _():
        for cp in copies(step + 1, 1 - slot):
            cp.start()

    for cp in copies(step, slot):
        cp.wait()

    query = lax.broadcasted_iota(jnp.int32, (1, dec_seq), 1)
    q_col = jnp.sum(jnp.where(query == i, q_ref[...], 0.0), axis=-1, keepdims=True) * (dh ** -0.5)
    col_f = lax.broadcasted_iota(jnp.int32, (1, MOBA_BLOCK), 1).astype(F32)
    own = lax.broadcasted_iota(jnp.int32, (1, kn_ref.shape[1]), 1)
    outs = []
    for h in range(ATT_HEADS):
        hs = slice(h * dh, (h + 1) * dh)
        qh = q_col[hs]
        slope = 2.0 ** (-(h + 1))
        so = jnp.sum(kn_ref[hs, :] * qh, axis=0, keepdims=True) + slope * own.astype(F32)
        so = jnp.where(own <= i, so, NEG_INF)
        m = jnp.max(so, axis=-1, keepdims=True)
        s_blocks = []
        for r in range(n_sel):
            rel = (picked_block(step, h, r) * MOBA_BLOCK - past_len).astype(F32)
            s = jnp.sum(kbuf[slot, h * n_sel + r] * qh, axis=0, keepdims=True) + slope * (col_f + rel)
            s_blocks.append(s)
            m = jnp.maximum(m, jnp.max(s, axis=-1, keepdims=True))
        p = jnp.exp(so - m)
        l = jnp.sum(p, axis=-1, keepdims=True)
        acc = jnp.sum(vn_ref[hs, :] * p, axis=-1, keepdims=True)
        weighted = jnp.zeros((dh, MOBA_BLOCK), F32)
        for r in range(n_sel):
            p = jnp.exp(s_blocks[r] - m)
            l = l + jnp.sum(p, axis=-1, keepdims=True)
            weighted = weighted + vbuf[slot, h * n_sel + r] * p
        outs.append((acc + jnp.sum(weighted, axis=-1, keepdims=True)) / l)
    o_col = jnp.concatenate(outs, axis=0)

    @pl.when(i == 0)
    def _():
        o_ref[...] = jnp.zeros(o_ref.shape, F32)

    o_ref[...] = jnp.where(query == i, o_col, o_ref[...])


def _moba_sample(q, k_new, v_new, cache_k, cache_v, page_table):
    db, dec_seq, _ = q.shape
    n_pages = page_table.shape[1]
    nblk = n_pages * PAGE_SIZE // MOBA_BLOCK
    n_sel = min(MOBA_TOPK, nblk)
    n_combo = ATT_HEADS * n_sel
    new_pad = 8
    pool_t = lambda c: c.transpose(0, 2, 3, 1)
    tok_t = lambda a, pad: jnp.pad(a.transpose(0, 2, 1), ((0, 0), (0, 0), (0, pad)))
    means_t = _pool_means(pool_t(cache_k), page_table, 8)
    idx = _select_blocks(q, means_t, n_sel)
    tspec = lambda n: pl.BlockSpec((None, ATT_WIDTH, n), lambda b, i, pt, ix: (b, 0, 0))
    grid_spec = pltpu.PrefetchScalarGridSpec(
        num_scalar_prefetch=2,
        grid=(db, dec_seq),
        in_specs=[tspec(dec_seq), tspec(new_pad), tspec(new_pad),
                  pl.BlockSpec(memory_space=pl.ANY), pl.BlockSpec(memory_space=pl.ANY)],
        out_specs=tspec(dec_seq),
        scratch_shapes=[pltpu.VMEM((2, n_combo, ATT_HEAD_DIM, MOBA_BLOCK), F32),
                        pltpu.VMEM((2, n_combo, ATT_HEAD_DIM, MOBA_BLOCK), F32),
                        pltpu.SemaphoreType.DMA((2,))],
    )
    out_t = pl.pallas_call(
        functools.partial(_gather_attn_kernel, n_pages=n_pages, dec_seq=dec_seq, n_sel=n_sel),
        grid_spec=grid_spec,
        out_shape=jax.ShapeDtypeStruct((db, ATT_WIDTH, dec_seq), F32),
        compiler_params=_params("arbitrary", "arbitrary"),
        name="moba_sample",
    )(page_table.reshape(-1), idx.reshape(-1), tok_t(q, 0), tok_t(k_new, new_pad - dec_seq),
      tok_t(v_new, new_pad - dec_seq), pool_t(cache_k), pool_t(cache_v))
    return out_t.transpose(0, 2, 1).reshape(db * dec_seq, ATT_WIDTH)


def _hgrn_kernel(hg_ref, lbl_ref, g_ref, s0_ref, o_ref, sout_ref,
                 st_ref, qa_ref, cum_ref, kk_ref, raw_ref, qf_ref, kf_ref, vb_ref, *, tc, n_valid):
    t = pl.program_id(1)
    w = HG_WIDTH
    c = HG_SUB
    nc = tc // c
    slab = min(tc, LANES)

    @pl.when(t == 0)
    def _():
        for h in range(HG_HEADS):
            st_ref[h] = s0_ref[h].T

    lbl = lbl_ref[...]
    e = jnp.exp(lbl - jnp.max(lbl, axis=0, keepdims=True))
    lb = e[0:1, :] / jnp.sum(e, axis=0, keepdims=True)
    f = lb + (1.0 - lb) * jax.nn.sigmoid(hg_ref[:, w:2 * w])
    lf = jnp.log(f)
    kk = 1.0 - f
    if n_valid < tc:
        valid = lax.broadcasted_iota(jnp.int32, (tc, 1), 0) < n_valid
        lf = jnp.where(valid, lf, 0.0)
        kk = jnp.where(valid, kk, 0.0)
    row = lax.broadcasted_iota(jnp.int32, (tc, 1), 0)
    pos = row % c
    cum = lf
    shift = 1
    while shift < c:
        cum = cum + jnp.where(pos >= shift, pltpu.roll(cum, shift, axis=0), 0.0)
        shift *= 2
    q = _silu(hg_ref[:, 0:w])
    last = jnp.broadcast_to(cum.reshape(nc, c, w)[:, c - 1:c, :], (nc, c, w)).reshape(tc, w)
    qa_ref[...] = q
    kk_ref[...] = kk
    cum_ref[...] = cum
    qf_ref[...] = (q * jnp.exp(cum)).astype(BF16)
    kf_ref[...] = (kk * jnp.exp(last - cum)).astype(BF16)
    vb_ref[...] = hg_ref[:, 2 * w:3 * w].astype(BF16)

    chunk_col = (row % slab) // c * c
    lane = lax.broadcasted_iota(jnp.int32, (1, slab), 1)
    sub = lax.broadcasted_iota(jnp.int32, (1, c, 1), 1)
    for h in range(HG_HEADS):
        cs = slice(h * HG_DK, (h + 1) * HG_DK)
        q3 = qa_ref[:, cs].reshape(nc, c, HG_DK)
        k3 = kk_ref[:, cs].reshape(nc, c, HG_DK)
        cum3 = cum_ref[:, cs].reshape(nc, c, HG_DK)
        attn = jnp.zeros((tc, slab), F32)
        for s in range(c):
            d = jnp.where(sub >= s, cum3 - cum3[:, s:s + 1, :], NEG_INF)
            a_col = jnp.sum(q3 * k3[:, s:s + 1, :] * jnp.exp(d), axis=-1, keepdims=True)
            attn = jnp.where(lane == chunk_col + s, a_col.reshape(tc, 1), attn)
        for g in range(tc // slab):
            rows = slice(g * slab, (g + 1) * slab)
            raw_ref[rows, cs] = _dot(attn[rows].astype(BF16), vb_ref[rows, cs])

    def chunk(ci, carry):
        r0 = pl.multiple_of(ci * c, c)
        for h in range(HG_HEADS):
            cs = slice(h * HG_DK, (h + 1) * HG_DK)
            s_t = st_ref[h]
            decay = jnp.exp(cum_ref[pl.ds(r0, c), cs][c - 1:c, :])
            raw_ref[pl.ds(r0, c), cs] += _dot_nt(qf_ref[pl.ds(r0, c), cs], s_t.astype(BF16))
            st_ref[h] = s_t * decay + _dot_tn(vb_ref[pl.ds(r0, c), cs], kf_ref[pl.ds(r0, c), cs])
        return carry

    lax.fori_loop(0, nc, chunk, 0, unroll=min(nc, 2))

    for h in range(HG_HEADS):
        cs = slice(h * HG_DV, (h + 1) * HG_DV)
        gate = _silu(hg_ref[:, 3 * w + h * HG_DV:3 * w + (h + 1) * HG_DV])
        o_ref[:, cs] = _rms(raw_ref[:, cs]) * g_ref[:, cs] * gate

    @pl.when(t == pl.num_programs(1) - 1)
    def _():
        for h in range(HG_HEADS):
            sout_ref[h] = st_ref[h].T


def _hgrn(hg, lb_logits, g, s0, batch, tc, n_valid):
    n = hg.shape[0]
    n_t = n // (batch * tc)
    state_spec = pl.BlockSpec((None, HG_HEADS, HG_DK, HG_DV), lambda b, t: (b, 0, 0, 0))
    return pl.pallas_call(
        functools.partial(_hgrn_kernel, tc=tc, n_valid=n_valid),
        grid=(batch, n_t),
        in_specs=[pl.BlockSpec((tc, hg.shape[1]), lambda b, t: (b * n_t + t, 0)),
                  pl.BlockSpec(lb_logits.shape, lambda b, t: (0, 0)),
                  pl.BlockSpec(g.shape, lambda b, t: (0, 0)),
                  state_spec],
        out_specs=[pl.BlockSpec((tc, HG_WIDTH), lambda b, t: (b * n_t + t, 0)), state_spec],
        out_shape=[jax.ShapeDtypeStruct((n, HG_WIDTH), F32),
                   jax.ShapeDtypeStruct((batch, HG_HEADS, HG_DK, HG_DV), F32)],
        scratch_shapes=[pltpu.VMEM((HG_HEADS, HG_DV, HG_DK), F32)]
        + [pltpu.VMEM((tc, HG_WIDTH), F32)] * 4 + [pltpu.VMEM((tc, HG_WIDTH), BF16)] * 3,
        compiler_params=_params("parallel", "arbitrary"),
        name="hgrn2",
    )(hg, lb_logits, g, s0)


def _route(h2, rwt, rb):
    tm = h2.shape[0]
    ng, gsz = N_EXPERT_GROUPS, GROUP_SIZE
    scores = jax.nn.sigmoid(_dot_nt(rwt, h2, precision=HI))
    s3 = scores.reshape(ng, gsz, tm)
    b3 = (scores + rb).reshape(ng, gsz, tm)
    sub = lax.broadcasted_iota(jnp.int32, (1, gsz, 1), 1).astype(F32)
    grp = lax.broadcasted_iota(jnp.int32, (ng, 1, 1), 0).astype(F32)
    m1 = jnp.max(b3, axis=1, keepdims=True)
    i1 = jnp.min(jnp.where(b3 == m1, sub, float(gsz)), axis=1, keepdims=True)
    m2 = jnp.max(jnp.where(sub == i1, TAKEN, b3), axis=1, keepdims=True)
    gs = m1 + m2
    beaten = jnp.zeros(gs.shape, F32)
    for g in range(ng):
        other = gs[g:g + 1]
        wins = jnp.logical_or(other > gs, jnp.logical_and(other == gs, grp > float(g)))
        beaten = beaten + wins.astype(F32)
    keep = beaten < float(TOPK_GROUPS)
    cand = jnp.where(keep, b3, NEG_INF)
    eidx = grp * float(gsz) + sub
    sel = jnp.zeros(cand.shape, jnp.bool_)
    for _ in range(EXPERT_TOPK):
        m = jnp.max(jnp.max(cand, axis=1, keepdims=True), axis=0, keepdims=True)
        hit_idx = jnp.where(cand == m, eidx, float(N_EXPERTS))
        idx = jnp.min(jnp.min(hit_idx, axis=1, keepdims=True), axis=0, keepdims=True)
        hit = eidx == idx
        sel = jnp.logical_or(sel, hit)
        cand = jnp.where(hit, TAKEN, cand)
    wsel = jnp.where(sel, s3, 0.0)
    tot = jnp.sum(jnp.sum(wsel, axis=1, keepdims=True), axis=0, keepdims=True)
    return (wsel / tot * ROUTED_SCALE).reshape(N_EXPERTS, tm)


def _outproj_kernel(x_ref, oa_ref, oh_ref, ag_ref, wo_ref, g1_ref, sh2_ref, sc2_ref, n2_ref,
                    rwt_ref, rb_ref, x1_ref, h2_ref, gates_ref):
    oa = (_rms(oa_ref[...]) * ag_ref[...]).astype(BF16)
    y = _dot(oa, wo_ref[0:ATT_WIDTH, :]) + _dot(oh_ref[...].astype(BF16), wo_ref[ATT_WIDTH:, :])
    x1 = x_ref[...] + g1_ref[...] * y
    x1_ref[...] = x1
    h2 = _rms(x1) * n2_ref[...]
    h2 = h2 * (1.0 + sc2_ref[...]) + sh2_ref[...]
    h2_ref[...] = h2.astype(BF16)
    gates_t = _route(h2, rwt_ref[...], rb_ref[...])
    pad = jnp.zeros((LANES - N_EXPERTS, gates_t.shape[1]), F32)
    gates_ref[...] = jnp.concatenate([gates_t, pad], axis=0).T


def _outproj(x, oa, oh, ag, wo_bf, g1, sh2, sc2, n2, rwt, rb, tm, tiles_per_batch):
    n, d = x.shape
    row = lambda c: pl.BlockSpec((tm, c), lambda i: (i, 0))
    mod = lambda a: _mod_spec(a, tm, tiles_per_batch, 1)
    return pl.pallas_call(
        _outproj_kernel,
        grid=(n // tm,),
        in_specs=[row(d), row(ATT_WIDTH), row(HG_WIDTH), _full_spec(ag, 1), _full_spec(wo_bf, 1),
                  mod(g1), mod(sh2), mod(sc2), _full_spec(n2, 1), _full_spec(rwt, 1), _full_spec(rb, 1)],
        out_specs=[row(d), row(d), row(LANES)],
        out_shape=[jax.ShapeDtypeStruct((n, d), F32), jax.ShapeDtypeStruct((n, d), BF16),
                   jax.ShapeDtypeStruct((n, LANES), F32)],
        compiler_params=_params("parallel"),
        name="out_proj_route",
    )(x, oa, oh, ag, wo_bf, g1, sh2, sc2, n2, rwt, rb)


def _moe_kernel(h2_ref, gates_ref, x1_ref, g2_ref, shf_ref, scf_ref, fg_ref,
                w1_ref, w3_ref, w2_ref, sw1_ref, sw3_ref, sw2_ref, y_ref, *, sub):
    g = pl.program_id(1)
    tm = h2_ref.shape[0]
    lane = lax.broadcasted_iota(jnp.int32, (1, LANES), 1)

    @pl.when(g == 0)
    def _():
        for r in range(0, tm, sub):
            h = h2_ref[r:r + sub, :]
            act = _silu(_dot(h, sw1_ref[...])) * _dot(h, sw3_ref[...])
            y_ref[r:r + sub, :] = _dot(act.astype(BF16), sw2_ref[...])

    for r in range(0, tm, sub):
        h = h2_ref[r:r + sub, :]
        gates = gates_ref[r:r + sub, :]
        acts = []
        for k in range(MOE_GROUP):
            gate = jnp.sum(jnp.where(lane == g * MOE_GROUP + k, gates, 0.0), axis=-1, keepdims=True)
            act = _silu(_dot(h, w1_ref[k])) * _dot(h, w3_ref[k]) * gate
            acts.append(act.astype(BF16))
        y_ref[r:r + sub, :] += _dot(jnp.concatenate(acts, axis=-1), w2_ref[...])

    @pl.when(g == pl.num_programs(1) - 1)
    def _():
        x2 = x1_ref[...] + g2_ref[...] * y_ref[...]
        y_ref[...] = _rms(x2) * fg_ref[...] * (1.0 + scf_ref[...]) + shf_ref[...]


def _moe(h2, gates, x1, g2, shf, scf, fg, w1, w3, w2, sw1, sw3, sw2, tm, tiles_per_batch):
    n, d = x1.shape
    row = lambda c: pl.BlockSpec((tm, c), lambda i, g: (i, 0))
    mod = lambda a: _mod_spec(a, tm, tiles_per_batch, 2)
    wspec = lambda a: pl.BlockSpec((None,) + a.shape[1:], lambda i, g: (g,) + (0,) * (a.ndim - 1))
    return pl.pallas_call(
        functools.partial(_moe_kernel, sub=min(tm, 256)),
        grid=(n // tm, w1.shape[0]),
        in_specs=[row(d), row(LANES), row(d), mod(g2), mod(shf), mod(scf), _full_spec(fg, 2),
                  wspec(w1), wspec(w3), wspec(w2), _full_spec(sw1, 2), _full_spec(sw3, 2), _full_spec(sw2, 2)],
        out_specs=row(d),
        out_shape=jax.ShapeDtypeStruct((n, d), F32),
        compiler_params=_params("parallel", "arbitrary"),
        name="moe_final",
    )(h2, gates, x1, g2, shf, scf, fg, w1, w3, w2, sw1, sw3, sw2)


def _split_mod(mod, n_parts, per_token_repeat):
    parts = jnp.split(mod, n_parts, axis=-1)
    if per_token_repeat is None:
        return [p[:, None, :] for p in parts]
    return [jnp.repeat(p, per_token_repeat, axis=0) for p in parts]


def kernel(x_prompt, x_sample, c_prompt, c_sample, cache_k, cache_v, state_hgrn, page_table, ada_w, ada_b, norm1_g, norm2_g, w_in, att_norm_g, hg_norm_g, hg_lb_logits, w_out, router_w, router_bias, exp_w1, exp_w3, exp_w2, shared_w1, shared_w3, shared_w2, final_g, ada_final_w, ada_final_b):
    assert ada_w.shape[0] == 1, "single trunk layer"
    batch, seq, d = x_prompt.shape
    db, dec_seq, _ = x_sample.shape
    n_p, n_s = batch * seq, db * dec_seq

    c_all = jnp.concatenate([c_prompt, c_sample], axis=0)
    c_rows = -(-c_all.shape[0] // 8) * 8
    c_all = jnp.pad(c_all, ((0, c_rows - c_all.shape[0]), (0, 0)))
    mod = _ada(c_all, ada_w[0], ada_b[0], 1024)
    modf = _ada(c_all, ada_final_w, ada_final_b, 1024)
    mods_p = _split_mod(mod[:batch], 6, None) + _split_mod(modf[:batch], 2, None)
    mods_s = _split_mod(mod[batch:batch + db], 6, dec_seq) + _split_mod(modf[batch:batch + db], 2, dec_seq)

    a = ATT_WIDTH
    w_qhg = jnp.concatenate([w_in[0][:, :a], w_in[0][:, 3 * a:]], axis=1).astype(BF16)
    w_kvt = w_in[0][:, a:3 * a].T.astype(BF16)
    w_out_bf = w_out[0].astype(BF16)
    n_grp = N_EXPERTS // MOE_GROUP
    w1 = exp_w1[0].astype(BF16).reshape(n_grp, MOE_GROUP, d, EXPERT_DIM)
    w3 = exp_w3[0].astype(BF16).reshape(n_grp, MOE_GROUP, d, EXPERT_DIM)
    w2 = exp_w2[0].astype(BF16).reshape(n_grp, MOE_GROUP * EXPERT_DIM, d)
    sw1, sw3, sw2 = shared_w1[0].astype(BF16), shared_w3[0].astype(BF16), shared_w2[0].astype(BF16)
    n1, n2 = norm1_g[0].reshape(1, d), norm2_g[0].reshape(1, d)
    ag, hgg = att_norm_g[0].reshape(1, ATT_WIDTH), hg_norm_g[0].reshape(1, HG_WIDTH)
    fg = final_g.reshape(1, d)
    rwt = router_w[0].T
    rb = router_bias[0].reshape(N_EXPERTS, 1)
    lbl = hg_lb_logits.astype(F32)

    def tail(x2d, oa, oh, mods, tm, tiles_per_batch, tm_moe, tiles_per_batch_moe):
        sh1, sc1, g1, sh2, sc2, g2, shf, scf = mods
        x1, h2, gates = _outproj(x2d, oa, oh, ag, w_out_bf, g1, sh2, sc2, n2, rwt, rb, tm, tiles_per_batch)
        return _moe(h2, gates, x1, g2, shf, scf, fg, w1, w3, w2, sw1, sw3, sw2, tm_moe, tiles_per_batch_moe)

    xp = x_prompt.reshape(n_p, d)
    tm_p = 512
    q_p, kt_p, vt_p, hg_p = _inproj(xp, mods_p[0], mods_p[1], n1, w_qhg, w_kvt, tm_p, seq // tm_p)
    oa_p = _moba_prompt(q_p, kt_p, vt_p, batch, seq)
    s0_p = jnp.zeros((batch, HG_HEADS, HG_DK, HG_DV), F32)
    oh_p, st_p = _hgrn(hg_p, lbl, hgg, s0_p, batch, 512, 512)
    y_p = tail(xp, oa_p, oh_p, mods_p, tm_p, seq // tm_p, 1024, seq // 1024)

    xs = x_sample.reshape(n_s, d)
    q_s, kt_s, vt_s, hg_s = _inproj(xs, mods_s[0], mods_s[1], n1, w_qhg, w_kvt, n_s, 1)
    k_s, v_s = kt_s[0].T, vt_s[0].T
    tok3 = lambda a: a.reshape(db, dec_seq, ATT_WIDTH)
    oa_s = _moba_sample(tok3(q_s), tok3(k_s), tok3(v_s), cache_k[0], cache_v[0], page_table)
    hg_s_pad = jnp.pad(hg_s.reshape(db, dec_seq, -1), ((0, 0), (0, HG_SUB - dec_seq), (0, 0)))
    oh_s, st_s = _hgrn(hg_s_pad.reshape(db * HG_SUB, -1), lbl, hgg, state_hgrn[0], db, HG_SUB, dec_seq)
    oh_s = oh_s.reshape(db, HG_SUB, HG_WIDTH)[:, :dec_seq].reshape(n_s, HG_WIDTH)
    y_s = tail(xs, oa_s, oh_s, mods_s, n_s, 1, n_s, 1)

    kv_p = lambda t: t.reshape(batch, ATT_HEADS, ATT_HEAD_DIM, seq).transpose(0, 3, 1, 2)[None]
    kv_s = (1, db, dec_seq, ATT_HEADS, ATT_HEAD_DIM)
    return (y_p.reshape(batch, seq, d), y_s.reshape(db, dec_seq, d),
            kv_p(kt_p), kv_p(vt_p), st_p[None],
            k_s.reshape(kv_s), v_s.reshape(kv_s), st_s[None])
```

```python
import functools

import jax
import jax.numpy as jnp
from jax import lax
from jax.experimental import pallas as pl
from jax.experimental.pallas import tpu as pltpu

F32 = jnp.float32
BF16 = jnp.bfloat16
HI = lax.Precision.HIGHEST

D_MODEL = 1024
PAGE_SIZE = 128
ATT_HEADS = 8
ATT_HEAD_DIM = 64
ATT_WIDTH = ATT_HEADS * ATT_HEAD_DIM
MOBA_BLOCK = 256
MOBA_TOPK = 3
PV_ROWS = ATT_HEAD_DIM + 16
HG_HEADS = 4
HG_DK = 128
HG_DV = 128
HG_WIDTH = HG_HEADS * HG_DV
HG_SUB = 16
N_EXPERTS = 64
EXPERT_TOPK = 6
N_EXPERT_GROUPS = 8
GROUP_SIZE = N_EXPERTS // N_EXPERT_GROUPS
TOPK_GROUPS = 4
EXPERT_DIM = 256
MOE_GROUP = 8
ROUTED_SCALE = 2.5
RMS_EPS = 1e-6
NEG_INF = -1e30
TAKEN = -3e38
LOG2E = 1.4426950408889634
LANES = 128
VMEM_LIMIT = 56 * 1024 * 1024


def _silu(x):
    return x * jax.nn.sigmoid(x)


def _rms(x):
    return x * lax.rsqrt(jnp.mean(x * x, axis=-1, keepdims=True) + RMS_EPS)


def _dot_nt(a, b, **kw):
    return lax.dot_general(a, b, (((1,), (1,)), ((), ())), preferred_element_type=F32, **kw)


def _dot_tn(a, b, **kw):
    return lax.dot_general(a, b, (((0,), (0,)), ((), ())), preferred_element_type=F32, **kw)


def _dot(a, b, **kw):
    return jnp.dot(a, b, preferred_element_type=F32, **kw)


def _params(*sem):
    return pltpu.CompilerParams(dimension_semantics=sem, vmem_limit_bytes=VMEM_LIMIT)


def _mod_spec(arr, tm, tiles_per_batch, grid_rank):
    if arr.ndim == 3:
        if grid_rank == 1:
            return pl.BlockSpec((None, 1, arr.shape[-1]), lambda i: (i // tiles_per_batch, 0, 0))
        return pl.BlockSpec((None, 1, arr.shape[-1]), lambda i, e: (i // tiles_per_batch, 0, 0))
    if grid_rank == 1:
        return pl.BlockSpec((tm, arr.shape[-1]), lambda i: (i, 0))
    return pl.BlockSpec((tm, arr.shape[-1]), lambda i, e: (i, 0))


def _full_spec(arr, grid_rank):
    zeros = (0,) * arr.ndim
    if grid_rank == 1:
        return pl.BlockSpec(arr.shape, lambda i: zeros)
    return pl.BlockSpec(arr.shape, lambda i, e: zeros)


def _ada_kernel(c_ref, w_ref, b_ref, o_ref):
    o_ref[...] = _dot(_silu(c_ref[...]), w_ref[...], precision=HI) + b_ref[...]


def _ada(c, w, b, tn):
    m, d = c.shape
    n = w.shape[1]
    return pl.pallas_call(
        _ada_kernel,
        grid=(n // tn,),
        in_specs=[pl.BlockSpec((m, d), lambda j: (0, 0)),
                  pl.BlockSpec((d, tn), lambda j: (0, j)),
                  pl.BlockSpec((1, tn), lambda j: (0, j))],
        out_specs=pl.BlockSpec((m, tn), lambda j: (0, j)),
        out_shape=jax.ShapeDtypeStruct((m, n), F32),
        compiler_params=_params("parallel"),
        name="ada_mod",
    )(c, w, b.reshape(1, n))


def _inproj_kernel(x_ref, sh_ref, sc_ref, g_ref, w_ref, wkvt_ref, q_ref, kt_ref, vt_ref, hg_ref):
    h = _rms(x_ref[...]) * g_ref[...]
    h = (h * (1.0 + sc_ref[...]) + sh_ref[...]).astype(BF16)
    a = ATT_WIDTH
    q_ref[...] = _dot(h, w_ref[:, 0:a])
    kt_ref[...] = _dot_nt(wkvt_ref[0:a, :], h)
    vt_ref[...] = _dot_nt(wkvt_ref[a:2 * a, :], h)
    hg_ref[...] = _dot(h, w_ref[:, a:])


def _inproj(x, sh, sc, g, w_qhg, w_kvt, tm, tiles_per_batch):
    n, d = x.shape
    hg_cols = w_qhg.shape[1] - ATT_WIDTH
    n_batches = n // (tm * tiles_per_batch)
    row = lambda c: pl.BlockSpec((tm, c), lambda i: (i, 0))
    tspec = pl.BlockSpec((None, ATT_WIDTH, tm), lambda i: (i // tiles_per_batch, 0, i % tiles_per_batch))
    t_shape = jax.ShapeDtypeStruct((n_batches, ATT_WIDTH, tm * tiles_per_batch), F32)
    return pl.pallas_call(
        _inproj_kernel,
        grid=(n // tm,),
        in_specs=[row(d), _mod_spec(sh, tm, tiles_per_batch, 1), _mod_spec(sc, tm, tiles_per_batch, 1),
                  _full_spec(g, 1), _full_spec(w_qhg, 1), _full_spec(w_kvt, 1)],
        out_specs=[row(ATT_WIDTH), tspec, tspec, row(hg_cols)],
        out_shape=[jax.ShapeDtypeStruct((n, ATT_WIDTH), F32), t_shape, t_shape,
                   jax.ShapeDtypeStruct((n, hg_cols), F32)],
        compiler_params=_params("parallel"),
        name="in_proj",
    )(x, sh, sc, g, w_qhg, w_kvt)


def _head_slope(head, shape):
    out = jnp.zeros(shape, F32)
    for i in range(ATT_HEADS):
        out = jnp.where(head == i, 2.0 ** (-(i + 1)), out)
    return out


def _top_blocks(gate, blk, n_sel, axis=-1):
    nb = gate.shape[axis]
    sel = jnp.zeros(gate.shape, jnp.bool_)
    for _ in range(n_sel):
        m = jnp.max(gate, axis=axis, keepdims=True)
        idx = jnp.min(jnp.where(gate == m, blk, float(nb)), axis=axis, keepdims=True)
        hit = blk == idx
        sel = jnp.logical_or(sel, hit)
        gate = jnp.where(hit, TAKEN, gate)
    return sel


def _block_means_step(pages, o_ref, s):
    n_blk_step = len(pages) // 2
    lane = lax.broadcasted_iota(jnp.int32, (1, o_ref.shape[1]), 1)

    @pl.when(s == 0)
    def _():
        o_ref[...] = jnp.zeros(o_ref.shape, F32)

    acc = o_ref[...]
    for i in range(n_blk_step):
        both = pages[2 * i][...] + pages[2 * i + 1][...]
        col = jnp.sum(both, axis=-1, keepdims=True).reshape(ATT_WIDTH, 1) * (1.0 / MOBA_BLOCK)
        acc = jnp.where(lane == s * n_blk_step + i, col, acc)
    o_ref[...] = acc


def _moba_prompt_kernel(pt_ref, q_ref, k_ref, v_ref, *refs, nb, n_pool_pages, pool_steps):
    del pt_ref
    pages, refs = refs[:n_pool_pages], refs[n_pool_pages:]
    n_out = 2 if n_pool_pages else 1
    o_ref = refs[0]
    means_ref, kbf_ref, vt_ref, base_ref, qst_ref, selt_ref, t_ref = refs[n_out:]
    hp = pl.program_id(1)
    qb = pl.program_id(2)
    bs = MOBA_BLOCK
    dh = ATT_HEAD_DIM
    if n_pool_pages:
        step = (pl.program_id(0) * pl.num_programs(1) + hp) * nb + qb
        _block_means_step(pages, refs[1], step % pool_steps)
    lane_head = lax.broadcasted_iota(jnp.int32, (1, 2 * bs), 1) // bs
    slope2 = _head_slope(2 * hp + lane_head, (1, 2 * bs)) * LOG2E

    @pl.when(qb == 0)
    def _():
        blk_lane = lax.broadcasted_iota(jnp.int32, (1, LANES), 1)
        ones_rows = (lax.broadcasted_iota(jnp.int32, (PV_ROWS - dh, bs), 0) == 0).astype(F32)

        def stage(j, means_t):
            j0 = pl.multiple_of(j * bs, bs)
            ktj = k_ref[:, pl.ds(j0, bs)]
            kbf_ref[j] = ktj.T.astype(BF16)
            vtj = v_ref[:, pl.ds(j0, bs)]
            vt_ref[j] = jnp.concatenate([vtj[:dh], ones_rows, vtj[dh:], ones_rows], axis=0).astype(BF16)
            return jnp.where(blk_lane == j, jnp.mean(ktj, axis=-1, keepdims=True), means_t)

        means_t = lax.fori_loop(0, nb, stage, jnp.zeros((LANES, LANES), F32))
        means_ref[...] = means_t.T[:nb]
        krow_f = lax.broadcasted_iota(jnp.int32, (bs, 2 * bs), 0).astype(F32)
        base_ref[...] = slope2 * krow_f

    qt = q_ref[...].T
    sub_head = lax.broadcasted_iota(jnp.int32, (LANES, 1), 0) // dh
    q2 = jnp.concatenate([jnp.where(sub_head == 0, qt, 0.0), jnp.where(sub_head == 1, qt, 0.0)], axis=1)
    blk = lax.broadcasted_iota(jnp.int32, (nb, 1), 0).astype(F32)
    past = blk < qb.astype(F32)
    gate = jnp.where(past, _dot(means_ref[...], q2, precision=HI), NEG_INF)
    sel = jnp.logical_and(_top_blocks(gate, blk, MOBA_TOPK, axis=0), past)
    selt_ref[...] = sel.astype(F32)
    qst_ref[...] = (q2 * (dh ** -0.5 * LOG2E)).astype(BF16)

    def scores(j, slot):
        t_ref[slot] = _dot(kbf_ref[j], qst_ref[...]) + base_ref[...]

    def weighted_values(j, p):
        vtj = vt_ref[j]
        pb = p.astype(BF16)
        return _dot(vtj[:PV_ROWS, :], pb[:, :bs]), _dot(vtj[PV_ROWS:, :], pb[:, bs:])

    scores(qb, 0)
    kq = lax.broadcasted_iota(jnp.int32, (bs, 2 * bs), 1) % bs
    causal = lax.broadcasted_iota(jnp.int32, (bs, 2 * bs), 0) <= kq
    s = jnp.where(causal, t_ref[0], NEG_INF)
    m = jnp.max(s, axis=0, keepdims=True)
    acc0, acc1 = weighted_values(qb, jnp.exp2(s - m))
    scores(0, 0)

    def attend(j, slot, carry):
        m, acc0, acc1 = carry
        jc = jnp.minimum(j, nb - 1)
        c = slope2 * (float(bs) * (qb - j).astype(F32))
        picked = jnp.logical_and(selt_ref[pl.ds(jc, 1), :] > 0.0, j < qb)
        m_blk = jnp.max(t_ref[slot], axis=0, keepdims=True) - c
        m_new = jnp.where(picked, jnp.maximum(m, m_blk), m)
        alpha = jnp.exp2(m - m_new)
        u = jnp.where(picked, m_new + c, -NEG_INF)
        pv0, pv1 = weighted_values(jc, jnp.exp2(t_ref[slot] - u))
        return m_new, alpha[:, :bs] * acc0 + pv0, alpha[:, bs:] * acc1 + pv1

    scores(jnp.minimum(1, nb - 1), 1)

    def body(i, carry):
        j = 2 * i
        carry = attend(j, 0, carry)
        scores(jnp.minimum(j + 2, nb - 1), 0)
        carry = attend(j + 1, 1, carry)
        scores(jnp.minimum(j + 3, nb - 1), 1)
        return carry

    m, acc0, acc1 = lax.fori_loop(0, (qb + 1) // 2, body, (m, acc0, acc1))
    o_t = jnp.concatenate([acc0[:dh] / acc0[dh:dh + 1], acc1[:dh] / acc1[dh:dh + 1]], axis=0)
    o_ref[...] = o_t.T


def _moba_prompt(q, k, v, batch, seq, pool_kt, page_table):
    nb = seq // MOBA_BLOCK
    n_hp = ATT_WIDTH // LANES
    assert nb <= LANES
    db, n_pages = page_table.shape
    nblk = n_pages * PAGE_SIZE // MOBA_BLOCK
    n_steps = batch * n_hp * nb
    pps = db * n_pages // n_steps
    pool_steps = n_pages // max(pps, 1)
    fused = (pps >= 2 and pps % 2 == 0 and pps * n_steps == db * n_pages
             and pool_steps * pps == n_pages and nb % pool_steps == 0)
    n_pool_pages = pps if fused else 0
    step = lambda b, hp, qb: (b * n_hp + hp) * nb + qb
    qspec = pl.BlockSpec((MOBA_BLOCK, LANES), lambda b, hp, qb, pt: (b * nb + qb, hp))
    kvspec = pl.BlockSpec((None, LANES, seq), lambda b, hp, qb, pt: (b, hp, 0))

    def page_spec(i):
        return pl.BlockSpec((None, ATT_HEADS, ATT_HEAD_DIM, PAGE_SIZE),
                            lambda b, hp, qb, pt: (pt[step(b, hp, qb) * pps + i], 0, 0, 0))

    out_specs, out_shape = [qspec], [jax.ShapeDtypeStruct(q.shape, F32)]
    if fused:
        out_specs.append(pl.BlockSpec((None, ATT_WIDTH, nblk), lambda b, hp, qb, pt: (step(b, hp, qb) // pool_steps, 0, 0)))
        out_shape.append(jax.ShapeDtypeStruct((db, ATT_WIDTH, nblk), F32))
    grid_spec = pltpu.PrefetchScalarGridSpec(
        num_scalar_prefetch=1,
        grid=(batch, n_hp, nb),
        in_specs=[qspec, kvspec, kvspec] + [page_spec(i) for i in range(n_pool_pages)],
        out_specs=out_specs,
        scratch_shapes=[pltpu.VMEM((nb, LANES), F32),
                        pltpu.VMEM((nb, MOBA_BLOCK, LANES), BF16),
                        pltpu.VMEM((nb, 2 * PV_ROWS, MOBA_BLOCK), BF16),
                        pltpu.VMEM((MOBA_BLOCK, 2 * MOBA_BLOCK), F32),
                        pltpu.VMEM((LANES, 2 * MOBA_BLOCK), BF16),
                        pltpu.VMEM((nb, 2 * MOBA_BLOCK), F32),
                        pltpu.VMEM((2, MOBA_BLOCK, 2 * MOBA_BLOCK), F32)],
    )
    outs = pl.pallas_call(
        functools.partial(_moba_prompt_kernel, nb=nb, n_pool_pages=n_pool_pages, pool_steps=pool_steps),
        grid_spec=grid_spec,
        out_shape=out_shape,
        compiler_params=_params("arbitrary", "arbitrary", "arbitrary"),
        name="moba_prompt",
    )(page_table.reshape(-1), q, k, v, *([pool_kt] * n_pool_pages))
    return outs[0], (outs[1] if fused else None)


def _pool_means_kernel(pt_ref, *refs, n_blk_step):
    del pt_ref
    _block_means_step(refs[:2 * n_blk_step], refs[2 * n_blk_step], pl.program_id(1))


def _pool_means(pool_kt, page_table, blocks_per_step):
    db, n_pages = page_table.shape
    nblk = n_pages * PAGE_SIZE // MOBA_BLOCK
    pps = 2 * blocks_per_step

    def page_spec(i):
        return pl.BlockSpec((None, ATT_HEADS, ATT_HEAD_DIM, PAGE_SIZE),
                            lambda b, s, pt: (pt[b * n_pages + s * pps + i], 0, 0, 0))

    grid_spec = pltpu.PrefetchScalarGridSpec(
        num_scalar_prefetch=1,
        grid=(db, n_pages // pps),
        in_specs=[page_spec(i) for i in range(pps)],
        out_specs=pl.BlockSpec((None, ATT_WIDTH, nblk), lambda b, s, pt: (b, 0, 0)),
    )
    return pl.pallas_call(
        functools.partial(_pool_means_kernel, n_blk_step=blocks_per_step),
        grid_spec=grid_spec,
        out_shape=jax.ShapeDtypeStruct((db, ATT_WIDTH, nblk), F32),
        compiler_params=_params("parallel", "arbitrary"),
        name="pool_means",
    )(page_table.reshape(-1), *([pool_kt] * pps))


def _select_kernel(q_ref, mt_ref, idx_ref, *, n_sel):
    q = q_ref[...]
    dec_seq = q.shape[0]
    nrow = dec_seq * ATT_HEADS
    qrep = jnp.broadcast_to(q[:, None, :], (dec_seq, ATT_HEADS, ATT_WIDTH)).reshape(nrow, ATT_WIDTH)
    row_head = lax.broadcasted_iota(jnp.int32, (nrow, 1), 0) % ATT_HEADS
    lane_head = lax.broadcasted_iota(jnp.int32, (1, ATT_WIDTH), 1) // ATT_HEAD_DIM
    gate = _dot(jnp.where(row_head == lane_head, qrep, 0.0), mt_ref[...], precision=HI)
    nblk = gate.shape[1]
    blk = lax.broadcasted_iota(jnp.int32, (1, nblk), 1).astype(F32)
    pick = lax.broadcasted_iota(jnp.int32, (1, n_sel), 1)
    out = jnp.zeros((nrow, n_sel), F32)
    for r in range(n_sel):
        m = jnp.max(gate, axis=-1, keepdims=True)
        idx = jnp.min(jnp.where(gate == m, blk, float(nblk)), axis=-1, keepdims=True)
        out = jnp.where(pick == r, idx, out)
        gate = jnp.where(blk == idx, TAKEN, gate)
    idx_ref[...] = out.astype(jnp.int32)


def _select_blocks(q3, means_t, n_sel):
    db, dec_seq, _ = q3.shape
    nblk = means_t.shape[2]
    nrow = dec_seq * ATT_HEADS
    return pl.pallas_call(
        functools.partial(_select_kernel, n_sel=n_sel),
        grid=(db,),
        in_specs=[pl.BlockSpec((None, dec_seq, ATT_WIDTH), lambda b: (b, 0, 0)),
                  pl.BlockSpec((None, ATT_WIDTH, nblk), lambda b: (b, 0, 0))],
        out_specs=pl.BlockSpec((None, nrow, n_sel), lambda b: (b, 0, 0)),
        out_shape=jax.ShapeDtypeStruct((db, nrow, n_sel), jnp.int32),
        compiler_params=_params("parallel"),
        name="moba_select",
    )(q3, means_t)


def _gather_attn_kernel(pt_ref, idx_ref, q_ref, kn_ref, vn_ref, pk_ref, pv_ref, o_ref, kbuf, vbuf, sem,
                        *, n_pages, dec_seq, n_sel):
    i = pl.program_id(1)
    step = pl.program_id(0) * dec_seq + i
    n_steps = pl.num_programs(0) * dec_seq
    slot = step % 2
    past_len = n_pages * PAGE_SIZE
    pages_per_block = MOBA_BLOCK // PAGE_SIZE
    dh = ATT_HEAD_DIM

    def picked_block(stp, h, r):
        return idx_ref[(stp * ATT_HEADS + h) * n_sel + r]

    def copies(stp, slt):
        bb = stp // dec_seq
        out = []
        for h in range(ATT_HEADS):
            for r in range(n_sel):
                blk = picked_block(stp, h, r)
                for pp in range(pages_per_block):
                    page = pt_ref[bb * n_pages + blk * pages_per_block + pp]
                    dst = pl.ds(pp * PAGE_SIZE, PAGE_SIZE)
                    c = h * n_sel + r
                    out.append(pltpu.make_async_copy(pk_ref.at[page, h], kbuf.at[slt, c, :, dst], sem.at[slt]))
                    out.append(pltpu.make_async_copy(pv_ref.at[page, h], vbuf.at[slt, c, :, dst], sem.at[slt]))
        return out

    @pl.when(step == 0)
    def _():
        for cp in copies(step, slot):
            cp.start()

    @pl.when(step + 1 < n_steps)
    def _():
        for cp in copies(step + 1, 1 - slot):
            cp.start()

    for cp in copies(step, slot):
        cp.wait()

    query = lax.broadcasted_iota(jnp.int32, (1, dec_seq), 1)
    q_col = jnp.sum(jnp.where(query == i, q_ref[...], 0.0), axis=-1, keepdims=True) * (dh ** -0.5)
    col_f = lax.broadcasted_iota(jnp.int32, (1, MOBA_BLOCK), 1).astype(F32)
    own = lax.broadcasted_iota(jnp.int32, (1, kn_ref.shape[1]), 1)
    outs = []
    for h in range(ATT_HEADS):
        hs = slice(h * dh, (h + 1) * dh)
        qh = q_col[hs]
        slope = 2.0 ** (-(h + 1))
        so = jnp.sum(kn_ref[hs, :] * qh, axis=0, keepdims=True) + slope * own.astype(F32)
        so = jnp.where(own <= i, so, NEG_INF)
        m = jnp.max(so, axis=-1, keepdims=True)
        s_blocks = []
        for r in range(n_sel):
            rel = (picked_block(step, h, r) * MOBA_BLOCK - past_len).astype(F32)
            s = jnp.sum(kbuf[slot, h * n_sel + r] * qh, axis=0, keepdims=True) + slope * (col_f + rel)
            s_blocks.append(s)
            m = jnp.maximum(m, jnp.max(s, axis=-1, keepdims=True))
        p = jnp.exp(so - m)
        l = jnp.sum(p, axis=-1, keepdims=True)
        acc = jnp.sum(vn_ref[hs, :] * p, axis=-1, keepdims=True)
        weighted = jnp.zeros((dh, MOBA_BLOCK), F32)
        for r in range(n_sel):
            p = jnp.exp(s_blocks[r] - m)
            l = l + jnp.sum(p, axis=-1, keepdims=True)
            weighted = weighted + vbuf[slot, h * n_sel + r] * p
        outs.append((acc + jnp.sum(weighted, axis=-1, keepdims=True)) / l)
    o_col = jnp.concatenate(outs, axis=0)

    @pl.when(i == 0)
    def _():
        o_ref[...] = jnp.zeros(o_ref.shape, F32)

    o_ref[...] = jnp.where(query == i, o_col, o_ref[...])


def _pool_view(cache):
    return cache.transpose(0, 2, 3, 1)


def _moba_sample(q, k_new, v_new, pool_kt, pool_vt, page_table, means_t):
    db, dec_seq, _ = q.shape
    n_pages = page_table.shape[1]
    nblk = n_pages * PAGE_SIZE // MOBA_BLOCK
    n_sel = min(MOBA_TOPK, nblk)
    n_combo = ATT_HEADS * n_sel
    new_pad = 8
    tok_t = lambda a, pad: jnp.pad(a.transpose(0, 2, 1), ((0, 0), (0, 0), (0, pad)))
    if means_t is None:
        means_t = _pool_means(pool_kt, page_table, 8)
    idx = _select_blocks(q, means_t, n_sel)
    tspec = lambda n: pl.BlockSpec((None, ATT_WIDTH, n), lambda b, i, pt, ix: (b, 0, 0))
    grid_spec = pltpu.PrefetchScalarGridSpec(
        num_scalar_prefetch=2,
        grid=(db, dec_seq),
        in_specs=[tspec(dec_seq), tspec(new_pad), tspec(new_pad),
                  pl.BlockSpec(memory_space=pl.ANY), pl.BlockSpec(memory_space=pl.ANY)],
        out_specs=tspec(dec_seq),
        scratch_shapes=[pltpu.VMEM((2, n_combo, ATT_HEAD_DIM, MOBA_BLOCK), F32),
                        pltpu.VMEM((2, n_combo, ATT_HEAD_DIM, MOBA_BLOCK), F32),
                        pltpu.SemaphoreType.DMA((2,))],
    )
    out_t = pl.pallas_call(
        functools.partial(_gather_attn_kernel, n_pages=n_pages, dec_seq=dec_seq, n_sel=n_sel),
        grid_spec=grid_spec,
        out_shape=jax.ShapeDtypeStruct((db, ATT_WIDTH, dec_seq), F32),
        compiler_params=_params("arbitrary", "arbitrary"),
        name="moba_sample",
    )(page_table.reshape(-1), idx.reshape(-1), tok_t(q, 0), tok_t(k_new, new_pad - dec_seq),
      tok_t(v_new, new_pad - dec_seq), pool_kt, pool_vt)
    return out_t.transpose(0, 2, 1).reshape(db * dec_seq, ATT_WIDTH)


def _hgrn_kernel(hg_ref, lbl_ref, g_ref, s0_ref, o_ref, sout_ref,
                 st_ref, qa_ref, cum_ref, kk_ref, raw_ref, qf_ref, kf_ref, vb_ref, *, tc, n_valid):
    t = pl.program_id(1)
    w = HG_WIDTH
    c = HG_SUB
    nc = tc // c
    slab = min(tc, LANES)

    @pl.when(t == 0)
    def _():
        for h in range(HG_HEADS):
            st_ref[h] = s0_ref[h].T

    lbl = lbl_ref[...]
    e = jnp.exp(lbl - jnp.max(lbl, axis=0, keepdims=True))
    lb = e[0:1, :] / jnp.sum(e, axis=0, keepdims=True)
    f = lb + (1.0 - lb) * jax.nn.sigmoid(hg_ref[:, w:2 * w])
    lf = jnp.log(f)
    kk = 1.0 - f
    if n_valid < tc:
        valid = lax.broadcasted_iota(jnp.int32, (tc, 1), 0) < n_valid
        lf = jnp.where(valid, lf, 0.0)
        kk = jnp.where(valid, kk, 0.0)
    row = lax.broadcasted_iota(jnp.int32, (tc, 1), 0)
    pos = row % c
    cum = lf
    shift = 1
    while shift < c:
        cum = cum + jnp.where(pos >= shift, pltpu.roll(cum, shift, axis=0), 0.0)
        shift *= 2
    q = _silu(hg_ref[:, 0:w])
    last = jnp.broadcast_to(cum.reshape(nc, c, w)[:, c - 1:c, :], (nc, c, w)).reshape(tc, w)
    qa_ref[...] = q
    kk_ref[...] = kk
    cum_ref[...] = cum
    qf_ref[...] = (q * jnp.exp(cum)).astype(BF16)
    kf_ref[...] = (kk * jnp.exp(last - cum)).astype(BF16)
    vb_ref[...] = hg_ref[:, 2 * w:3 * w].astype(BF16)

    chunk_col = (row % slab) // c * c
    lane = lax.broadcasted_iota(jnp.int32, (1, slab), 1)
    sub = lax.broadcasted_iota(jnp.int32, (1, c, 1), 1)
    for h in range(HG_HEADS):
        cs = slice(h * HG_DK, (h + 1) * HG_DK)
        q3 = qa_ref[:, cs].reshape(nc, c, HG_DK)
        k3 = kk_ref[:, cs].reshape(nc, c, HG_DK)
        cum3 = cum_ref[:, cs].reshape(nc, c, HG_DK)
        attn = jnp.zeros((tc, slab), F32)
        for s in range(c):
            d = jnp.where(sub >= s, cum3 - cum3[:, s:s + 1, :], NEG_INF)
            a_col = jnp.sum(q3 * k3[:, s:s + 1, :] * jnp.exp(d), axis=-1, keepdims=True)
            attn = jnp.where(lane == chunk_col + s, a_col.reshape(tc, 1), attn)
        for g in range(tc // slab):
            rows = slice(g * slab, (g + 1) * slab)
            raw_ref[rows, cs] = _dot(attn[rows].astype(BF16), vb_ref[rows, cs])

    def chunk(ci, carry):
        r0 = pl.multiple_of(ci * c, c)
        for h in range(HG_HEADS):
            cs = slice(h * HG_DK, (h + 1) * HG_DK)
            s_t = st_ref[h]
            decay = jnp.exp(cum_ref[pl.ds(r0, c), cs][c - 1:c, :])
            raw_ref[pl.ds(r0, c), cs] += _dot_nt(qf_ref[pl.ds(r0, c), cs], s_t.astype(BF16))
            st_ref[h] = s_t * decay + _dot_tn(vb_ref[pl.ds(r0, c), cs], kf_ref[pl.ds(r0, c), cs])
        return carry

    lax.fori_loop(0, nc, chunk, 0, unroll=min(nc, 2))

    for h in range(HG_HEADS):
        cs = slice(h * HG_DV, (h + 1) * HG_DV)
        gate = _silu(hg_ref[:, 3 * w + h * HG_DV:3 * w + (h + 1) * HG_DV])
        o_ref[:, cs] = _rms(raw_ref[:, cs]) * g_ref[:, cs] * gate

    @pl.when(t == pl.num_programs(1) - 1)
    def _():
        for h in range(HG_HEADS):
            sout_ref[h] = st_ref[h].T


def _hgrn(hg, lb_logits, g, s0, batch, tc, n_valid):
    n = hg.shape[0]
    n_t = n // (batch * tc)
    state_spec = pl.BlockSpec((None, HG_HEADS, HG_DK, HG_DV), lambda b, t: (b, 0, 0, 0))
    return pl.pallas_call(
        functools.partial(_hgrn_kernel, tc=tc, n_valid=n_valid),
        grid=(batch, n_t),
        in_specs=[pl.BlockSpec((tc, hg.shape[1]), lambda b, t: (b * n_t + t, 0)),
                  pl.BlockSpec(lb_logits.shape, lambda b, t: (0, 0)),
                  pl.BlockSpec(g.shape, lambda b, t: (0, 0)),
                  state_spec],
        out_specs=[pl.BlockSpec((tc, HG_WIDTH), lambda b, t: (b * n_t + t, 0)), state_spec],
        out_shape=[jax.ShapeDtypeStruct((n, HG_WIDTH), F32),
                   jax.ShapeDtypeStruct((batch, HG_HEADS, HG_DK, HG_DV), F32)],
        scratch_shapes=[pltpu.VMEM((HG_HEADS, HG_DV, HG_DK), F32)]
        + [pltpu.VMEM((tc, HG_WIDTH), F32)] * 4 + [pltpu.VMEM((tc, HG_WIDTH), BF16)] * 3,
        compiler_params=_params("parallel", "arbitrary"),
        name="hgrn2",
    )(hg, lb_logits, g, s0)


def _route(h2, rwt, rb):
    tm = h2.shape[0]
    ng, gsz = N_EXPERT_GROUPS, GROUP_SIZE
    scores = jax.nn.sigmoid(_dot_nt(rwt, h2, precision=HI))
    s3 = scores.reshape(ng, gsz, tm)
    b3 = (scores + rb).reshape(ng, gsz, tm)
    sub = lax.broadcasted_iota(jnp.int32, (1, gsz, 1), 1).astype(F32)
    grp = lax.broadcasted_iota(jnp.int32, (ng, 1, 1), 0).astype(F32)
    m1 = jnp.max(b3, axis=1, keepdims=True)
    i1 = jnp.min(jnp.where(b3 == m1, sub, float(gsz)), axis=1, keepdims=True)
    m2 = jnp.max(jnp.where(sub == i1, TAKEN, b3), axis=1, keepdims=True)
    gs = m1 + m2
    beaten = jnp.zeros(gs.shape, F32)
    for g in range(ng):
        other = gs[g:g + 1]
        wins = jnp.logical_or(other > gs, jnp.logical_and(other == gs, grp > float(g)))
        beaten = beaten + wins.astype(F32)
    keep = beaten < float(TOPK_GROUPS)
    cand = jnp.where(keep, b3, NEG_INF)
    eidx = grp * float(gsz) + sub
    sel = jnp.zeros(cand.shape, jnp.bool_)
    for _ in range(EXPERT_TOPK):
        m = jnp.max(jnp.max(cand, axis=1, keepdims=True), axis=0, keepdims=True)
        hit_idx = jnp.where(cand == m, eidx, float(N_EXPERTS))
        idx = jnp.min(jnp.min(hit_idx, axis=1, keepdims=True), axis=0, keepdims=True)
        hit = eidx == idx
        sel = jnp.logical_or(sel, hit)
        cand = jnp.where(hit, TAKEN, cand)
    wsel = jnp.where(sel, s3, 0.0)
    tot = jnp.sum(jnp.sum(wsel, axis=1, keepdims=True), axis=0, keepdims=True)
    return (wsel / tot * ROUTED_SCALE).reshape(N_EXPERTS, tm)


def _outproj_kernel(x_ref, oa_ref, oh_ref, ag_ref, wo_ref, g1_ref, sh2_ref, sc2_ref, n2_ref,
                    rwt_ref, rb_ref, x1_ref, h2_ref, gates_ref):
    oa = (_rms(oa_ref[...]) * ag_ref[...]).astype(BF16)
    y = _dot(oa, wo_ref[0:ATT_WIDTH, :]) + _dot(oh_ref[...].astype(BF16), wo_ref[ATT_WIDTH:, :])
    x1 = x_ref[...] + g1_ref[...] * y
    x1_ref[...] = x1
    h2 = _rms(x1) * n2_ref[...]
    h2 = h2 * (1.0 + sc2_ref[...]) + sh2_ref[...]
    h2_ref[...] = h2.astype(BF16)
    gates_t = _route(h2, rwt_ref[...], rb_ref[...])
    pad = jnp.zeros((LANES - N_EXPERTS, gates_t.shape[1]), F32)
    gates_ref[...] = jnp.concatenate([gates_t, pad], axis=0).T


def _outproj(x, oa, oh, ag, wo_bf, g1, sh2, sc2, n2, rwt, rb, tm, tiles_per_batch):
    n, d = x.shape
    row = lambda c: pl.BlockSpec((tm, c), lambda i: (i, 0))
    mod = lambda a: _mod_spec(a, tm, tiles_per_batch, 1)
    return pl.pallas_call(
        _outproj_kernel,
        grid=(n // tm,),
        in_specs=[row(d), row(ATT_WIDTH), row(HG_WIDTH), _full_spec(ag, 1), _full_spec(wo_bf, 1),
                  mod(g1), mod(sh2), mod(sc2), _full_spec(n2, 1), _full_spec(rwt, 1), _full_spec(rb, 1)],
        out_specs=[row(d), row(d), row(LANES)],
        out_shape=[jax.ShapeDtypeStruct((n, d), F32), jax.ShapeDtypeStruct((n, d), BF16),
                   jax.ShapeDtypeStruct((n, LANES), F32)],
        compiler_params=_params("parallel"),
        name="out_proj_route",
    )(x, oa, oh, ag, wo_bf, g1, sh2, sc2, n2, rwt, rb)


def _moe_kernel(h2_ref, gates_ref, x1_ref, g2_ref, shf_ref, scf_ref, fg_ref,
                w1_ref, w3_ref, w2_ref, sw1_ref, sw3_ref, sw2_ref, y_ref, *, sub):
    g = pl.program_id(1)
    tm = h2_ref.shape[0]
    lane = lax.broadcasted_iota(jnp.int32, (1, LANES), 1)

    @pl.when(g == 0)
    def _():
        for r in range(0, tm, sub):
            h = h2_ref[r:r + sub, :]
            act = _silu(_dot(h, sw1_ref[...])) * _dot(h, sw3_ref[...])
            y_ref[r:r + sub, :] = _dot(act.astype(BF16), sw2_ref[...])

    for r in range(0, tm, sub):
        h = h2_ref[r:r + sub, :]
        gates = gates_ref[r:r + sub, :]
        acts = []
        for k in range(MOE_GROUP):
            gate = jnp.sum(jnp.where(lane == g * MOE_GROUP + k, gates, 0.0), axis=-1, keepdims=True)
            act = _silu(_dot(h, w1_ref[k])) * _dot(h, w3_ref[k]) * gate
            acts.append(act.astype(BF16))
        y_ref[r:r + sub, :] += _dot(jnp.concatenate(acts, axis=-1), w2_ref[...])

    @pl.when(g == pl.num_programs(1) - 1)
    def _():
        x2 = x1_ref[...] + g2_ref[...] * y_ref[...]
        y_ref[...] = _rms(x2) * fg_ref[...] * (1.0 + scf_ref[...]) + shf_ref[...]


def _moe(h2, gates, x1, g2, shf, scf, fg, w1, w3, w2, sw1, sw3, sw2, tm, tiles_per_batch):
    n, d = x1.shape
    row = lambda c: pl.BlockSpec((tm, c), lambda i, g: (i, 0))
    mod = lambda a: _mod_spec(a, tm, tiles_per_batch, 2)
    wspec = lambda a: pl.BlockSpec((None,) + a.shape[1:], lambda i, g: (g,) + (0,) * (a.ndim - 1))
    return pl.pallas_call(
        functools.partial(_moe_kernel, sub=min(tm, 256)),
        grid=(n // tm, w1.shape[0]),
        in_specs=[row(d), row(LANES), row(d), mod(g2), mod(shf), mod(scf), _full_spec(fg, 2),
                  wspec(w1), wspec(w3), wspec(w2), _full_spec(sw1, 2), _full_spec(sw3, 2), _full_spec(sw2, 2)],
        out_specs=row(d),
        out_shape=jax.ShapeDtypeStruct((n, d), F32),
        compiler_params=_params("parallel", "arbitrary"),
        name="moe_final",
    )(h2, gates, x1, g2, shf, scf, fg, w1, w3, w2, sw1, sw3, sw2)


def _split_mod(mod, n_parts, per_token_repeat):
    parts = jnp.split(mod, n_parts, axis=-1)
    if per_token_repeat is None:
        return [p[:, None, :] for p in parts]
    return [jnp.repeat(p, per_token_repeat, axis=0) for p in parts]


def kernel(x_prompt, x_sample, c_prompt, c_sample, cache_k, cache_v, state_hgrn, page_table, ada_w, ada_b, norm1_g, norm2_g, w_in, att_norm_g, hg_norm_g, hg_lb_logits, w_out, router_w, router_bias, exp_w1, exp_w3, exp_w2, shared_w1, shared_w3, shared_w2, final_g, ada_final_w, ada_final_b):
    assert ada_w.shape[0] == 1, "single trunk layer"
    batch, seq, d = x_prompt.shape
    db, dec_seq, _ = x_sample.shape
    n_p, n_s = batch * seq, db * dec_seq

    c_all = jnp.concatenate([c_prompt, c_sample], axis=0)
    c_rows = -(-c_all.shape[0] // 8) * 8
    c_all = jnp.pad(c_all, ((0, c_rows - c_all.shape[0]), (0, 0)))
    mod = _ada(c_all, ada_w[0], ada_b[0], 1024)
    modf = _ada(c_all, ada_final_w, ada_final_b, 1024)
    mods_p = _split_mod(mod[:batch], 6, None) + _split_mod(modf[:batch], 2, None)
    mods_s = _split_mod(mod[batch:batch + db], 6, dec_seq) + _split_mod(modf[batch:batch + db], 2, dec_seq)

    a = ATT_WIDTH
    w_qhg = jnp.concatenate([w_in[0][:, :a], w_in[0][:, 3 * a:]], axis=1).astype(BF16)
    w_kvt = w_in[0][:, a:3 * a].T.astype(BF16)
    w_out_bf = w_out[0].astype(BF16)
    n_grp = N_EXPERTS // MOE_GROUP
    w1 = exp_w1[0].astype(BF16).reshape(n_grp, MOE_GROUP, d, EXPERT_DIM)
    w3 = exp_w3[0].astype(BF16).reshape(n_grp, MOE_GROUP, d, EXPERT_DIM)
    w2 = exp_w2[0].astype(BF16).reshape(n_grp, MOE_GROUP * EXPERT_DIM, d)
    sw1, sw3, sw2 = shared_w1[0].astype(BF16), shared_w3[0].astype(BF16), shared_w2[0].astype(BF16)
    n1, n2 = norm1_g[0].reshape(1, d), norm2_g[0].reshape(1, d)
    ag, hgg = att_norm_g[0].reshape(1, ATT_WIDTH), hg_norm_g[0].reshape(1, HG_WIDTH)
    fg = final_g.reshape(1, d)
    rwt = router_w[0].T
    rb = router_bias[0].reshape(N_EXPERTS, 1)
    lbl = hg_lb_logits.astype(F32)

    def tail(x2d, oa, oh, mods, tm, tiles_per_batch, tm_moe, tiles_per_batch_moe):
        sh1, sc1, g1, sh2, sc2, g2, shf, scf = mods
        x1, h2, gates = _outproj(x2d, oa, oh, ag, w_out_bf, g1, sh2, sc2, n2, rwt, rb, tm, tiles_per_batch)
        return _moe(h2, gates, x1, g2, shf, scf, fg, w1, w3, w2, sw1, sw3, sw2, tm_moe, tiles_per_batch_moe)

    xp = x_prompt.reshape(n_p, d)
    tm_p = 512
    q_p, kt_p, vt_p, hg_p = _inproj(xp, mods_p[0], mods_p[1], n1, w_qhg, w_kvt, tm_p, seq // tm_p)
    pool_kt, pool_vt = _pool_view(cache_k[0]), _pool_view(cache_v[0])
    oa_p, pool_means_t = _moba_prompt(q_p, kt_p, vt_p, batch, seq, pool_kt, page_table)
    s0_p = jnp.zeros((batch, HG_HEADS, HG_DK, HG_DV), F32)
    oh_p, st_p = _hgrn(hg_p, lbl, hgg, s0_p, batch, 512, 512)
    y_p = tail(xp, oa_p, oh_p, mods_p, tm_p, seq // tm_p, 1024, seq // 1024)

    xs = x_sample.reshape(n_s, d)
    q_s, kt_s, vt_s, hg_s = _inproj(xs, mods_s[0], mods_s[1], n1, w_qhg, w_kvt, n_s, 1)
    k_s, v_s = kt_s[0].T, vt_s[0].T
    tok3 = lambda a: a.reshape(db, dec_seq, ATT_WIDTH)
    oa_s = _moba_sample(tok3(q_s), tok3(k_s), tok3(v_s), pool_kt, pool_vt, page_table, pool_means_t)
    hg_s_pad = jnp.pad(hg_s.reshape(db, dec_seq, -1), ((0, 0), (0, HG_SUB - dec_seq), (0, 0)))
    oh_s, st_s = _hgrn(hg_s_pad.reshape(db * HG_SUB, -1), lbl, hgg, state_hgrn[0], db, HG_SUB, dec_seq)
    oh_s = oh_s.reshape(db, HG_SUB, HG_WIDTH)[:, :dec_seq].reshape(n_s, HG_WIDTH)
    y_s = tail(xs, oa_s, oh_s, mods_s, n_s, 1, n_s, 1)

    kv_p = lambda t: t.reshape(batch, ATT_HEADS, ATT_HEAD_DIM, seq).transpose(0, 3, 1, 2)[None]
    kv_s = (1, db, dec_seq, ATT_HEADS, ATT_HEAD_DIM)
    return (y_p.reshape(batch, seq, d), y_s.reshape(db, dec_seq, d),
            kv_p(kt_p), kv_p(vt_p), st_p[None],
            k_s.reshape(kv_s), v_s.reshape(kv_s), st_s[None])
```

```python
import functools

import jax
import jax.numpy as jnp
from jax import lax
from jax.experimental import pallas as pl
from jax.experimental.pallas import tpu as pltpu

F32 = jnp.float32
BF16 = jnp.bfloat16
HI = lax.Precision.HIGHEST

D_MODEL = 1024
PAGE_SIZE = 128
ATT_HEADS = 8
ATT_HEAD_DIM = 64
ATT_WIDTH = ATT_HEADS * ATT_HEAD_DIM
MOBA_BLOCK = 256
MOBA_TOPK = 3
PV_ROWS = ATT_HEAD_DIM + 16
HG_HEADS = 4
HG_DK = 128
HG_DV = 128
HG_WIDTH = HG_HEADS * HG_DV
HG_SUB = 16
N_EXPERTS = 64
EXPERT_TOPK = 6
N_EXPERT_GROUPS = 8
GROUP_SIZE = N_EXPERTS // N_EXPERT_GROUPS
TOPK_GROUPS = 4
EXPERT_DIM = 256
MOE_GROUP = 8
ROUTED_SCALE = 2.5
RMS_EPS = 1e-6
NEG_INF = -1e30
TAKEN = -3e38
LOG2E = 1.4426950408889634
LANES = 128
VMEM_LIMIT = 56 * 1024 * 1024


def _silu(x):
    return x * jax.nn.sigmoid(x)


def _rms(x):
    return x * lax.rsqrt(jnp.mean(x * x, axis=-1, keepdims=True) + RMS_EPS)


def _dot_nt(a, b, **kw):
    return lax.dot_general(a, b, (((1,), (1,)), ((), ())), preferred_element_type=F32, **kw)


def _dot_tn(a, b, **kw):
    return lax.dot_general(a, b, (((0,), (0,)), ((), ())), preferred_element_type=F32, **kw)


def _dot(a, b, **kw):
    return jnp.dot(a, b, preferred_element_type=F32, **kw)


def _params(*sem):
    return pltpu.CompilerParams(dimension_semantics=sem, vmem_limit_bytes=VMEM_LIMIT)


def _mod_spec(arr, tm, tiles_per_batch, grid_rank):
    if arr.ndim == 3:
        if grid_rank == 1:
            return pl.BlockSpec((None, 1, arr.shape[-1]), lambda i: (i // tiles_per_batch, 0, 0))
        return pl.BlockSpec((None, 1, arr.shape[-1]), lambda i, e: (i // tiles_per_batch, 0, 0))
    if grid_rank == 1:
        return pl.BlockSpec((tm, arr.shape[-1]), lambda i: (i, 0))
    return pl.BlockSpec((tm, arr.shape[-1]), lambda i, e: (i, 0))


def _full_spec(arr, grid_rank):
    zeros = (0,) * arr.ndim
    if grid_rank == 1:
        return pl.BlockSpec(arr.shape, lambda i: zeros)
    return pl.BlockSpec(arr.shape, lambda i, e: zeros)


def _ada_kernel(c_ref, w_ref, b_ref, o_ref):
    o_ref[...] = _dot(_silu(c_ref[...]), w_ref[...], precision=HI) + b_ref[...]


def _ada(c, w, b, tn):
    m, d = c.shape
    n = w.shape[1]
    return pl.pallas_call(
        _ada_kernel,
        grid=(n // tn,),
        in_specs=[pl.BlockSpec((m, d), lambda j: (0, 0)),
                  pl.BlockSpec((d, tn), lambda j: (0, j)),
                  pl.BlockSpec((1, tn), lambda j: (0, j))],
        out_specs=pl.BlockSpec((m, tn), lambda j: (0, j)),
        out_shape=jax.ShapeDtypeStruct((m, n), F32),
        compiler_params=_params("parallel"),
        name="ada_mod",
    )(c, w, b.reshape(1, n))


def _inproj_kernel(x_ref, sh_ref, sc_ref, g_ref, w_ref, wkvt_ref, q_ref, kt_ref, vt_ref, hg_ref):
    h = _rms(x_ref[...]) * g_ref[...]
    h = (h * (1.0 + sc_ref[...]) + sh_ref[...]).astype(BF16)
    a = ATT_WIDTH
    q_ref[...] = _dot(h, w_ref[:, 0:a])
    kt_ref[...] = _dot_nt(wkvt_ref[0:a, :], h)
    vt_ref[...] = _dot_nt(wkvt_ref[a:2 * a, :], h)
    hg_ref[...] = _dot(h, w_ref[:, a:])


def _inproj(x, sh, sc, g, w_qhg, w_kvt, tm, tiles_per_batch):
    n, d = x.shape
    hg_cols = w_qhg.shape[1] - ATT_WIDTH
    n_batches = n // (tm * tiles_per_batch)
    row = lambda c: pl.BlockSpec((tm, c), lambda i: (i, 0))
    tspec = pl.BlockSpec((None, ATT_WIDTH, tm), lambda i: (i // tiles_per_batch, 0, i % tiles_per_batch))
    t_shape = jax.ShapeDtypeStruct((n_batches, ATT_WIDTH, tm * tiles_per_batch), F32)
    return pl.pallas_call(
        _inproj_kernel,
        grid=(n // tm,),
        in_specs=[row(d), _mod_spec(sh, tm, tiles_per_batch, 1), _mod_spec(sc, tm, tiles_per_batch, 1),
                  _full_spec(g, 1), _full_spec(w_qhg, 1), _full_spec(w_kvt, 1)],
        out_specs=[row(ATT_WIDTH), tspec, tspec, row(hg_cols)],
        out_shape=[jax.ShapeDtypeStruct((n, ATT_WIDTH), F32), t_shape, t_shape,
                   jax.ShapeDtypeStruct((n, hg_cols), F32)],
        compiler_params=_params("parallel"),
        name="in_proj",
    )(x, sh, sc, g, w_qhg, w_kvt)


def _head_slope(head, shape):
    out = jnp.zeros(shape, F32)
    for i in range(ATT_HEADS):
        out = jnp.where(head == i, 2.0 ** (-(i + 1)), out)
    return out


def _top_blocks(gate, blk, n_sel, axis=-1):
    nb = gate.shape[axis]
    sel = jnp.zeros(gate.shape, jnp.bool_)
    for _ in range(n_sel):
        m = jnp.max(gate, axis=axis, keepdims=True)
        idx = jnp.min(jnp.where(gate == m, blk, float(nb)), axis=axis, keepdims=True)
        hit = blk == idx
        sel = jnp.logical_or(sel, hit)
        gate = jnp.where(hit, TAKEN, gate)
    return sel


def _block_means_step(pages, o_ref, s):
    n_blk_step = len(pages) // 2
    lane = lax.broadcasted_iota(jnp.int32, (1, o_ref.shape[1]), 1)

    @pl.when(s == 0)
    def _():
        o_ref[...] = jnp.zeros(o_ref.shape, F32)

    acc = o_ref[...]
    for i in range(n_blk_step):
        both = pages[2 * i][...] + pages[2 * i + 1][...]
        col = jnp.sum(both, axis=-1, keepdims=True).reshape(ATT_WIDTH, 1) * (1.0 / MOBA_BLOCK)
        acc = jnp.where(lane == s * n_blk_step + i, col, acc)
    o_ref[...] = acc


def _moba_prompt_kernel(pt_ref, q_ref, k_ref, v_ref, *refs, nb, n_pool_pages, pool_steps):
    del pt_ref
    pages, refs = refs[:n_pool_pages], refs[n_pool_pages:]
    n_out = 2 if n_pool_pages else 1
    o_ref = refs[0]
    means_ref, kbf_ref, vt_ref, base_ref, qst_ref, selt_ref, t_ref = refs[n_out:]
    hp = pl.program_id(1)
    qb = pl.program_id(2)
    bs = MOBA_BLOCK
    dh = ATT_HEAD_DIM
    lane_head = lax.broadcasted_iota(jnp.int32, (1, 2 * bs), 1) // bs
    slope2 = _head_slope(2 * hp + lane_head, (1, 2 * bs)) * LOG2E

    @pl.when(qb == 0)
    def _():
        blk_lane = lax.broadcasted_iota(jnp.int32, (1, LANES), 1)
        ones_rows = (lax.broadcasted_iota(jnp.int32, (PV_ROWS - dh, bs), 0) == 0).astype(F32)

        def stage(j, means_t):
            j0 = pl.multiple_of(j * bs, bs)
            ktj = k_ref[:, pl.ds(j0, bs)]
            kbf_ref[j] = ktj.T.astype(BF16)
            vtj = v_ref[:, pl.ds(j0, bs)]
            vt_ref[j] = jnp.concatenate([vtj[:dh], ones_rows, vtj[dh:], ones_rows], axis=0).astype(BF16)
            return jnp.where(blk_lane == j, jnp.mean(ktj, axis=-1, keepdims=True), means_t)

        means_t = lax.fori_loop(0, nb, stage, jnp.zeros((LANES, LANES), F32))
        means_ref[...] = means_t.T[:nb]
        krow_f = lax.broadcasted_iota(jnp.int32, (bs, 2 * bs), 0).astype(F32)
        base_ref[...] = slope2 * krow_f

    if n_pool_pages:
        step = (pl.program_id(0) * pl.num_programs(1) + hp) * nb + qb
        _block_means_step(pages, refs[1], step % pool_steps)

    qt = q_ref[...].T
    sub_head = lax.broadcasted_iota(jnp.int32, (LANES, 1), 0) // dh
    q2 = jnp.concatenate([jnp.where(sub_head == 0, qt, 0.0), jnp.where(sub_head == 1, qt, 0.0)], axis=1)
    blk = lax.broadcasted_iota(jnp.int32, (nb, 1), 0).astype(F32)
    past = blk < qb.astype(F32)
    gate = jnp.where(past, _dot(means_ref[...], q2, precision=HI), NEG_INF)
    sel = jnp.logical_and(_top_blocks(gate, blk, MOBA_TOPK, axis=0), past)
    selt_ref[...] = sel.astype(F32)
    qst_ref[...] = (q2 * (dh ** -0.5 * LOG2E)).astype(BF16)

    def scores(j, slot):
        t_ref[slot] = _dot(kbf_ref[j], qst_ref[...]) + base_ref[...]

    def weighted_values(j, p):
        vtj = vt_ref[j]
        pb = p.astype(BF16)
        return _dot(vtj[:PV_ROWS, :], pb[:, :bs]), _dot(vtj[PV_ROWS:, :], pb[:, bs:])

    scores(qb, 0)
    kq = lax.broadcasted_iota(jnp.int32, (bs, 2 * bs), 1) % bs
    causal = lax.broadcasted_iota(jnp.int32, (bs, 2 * bs), 0) <= kq
    s = jnp.where(causal, t_ref[0], NEG_INF)
    m = jnp.max(s, axis=0, keepdims=True)
    acc0, acc1 = weighted_values(qb, jnp.exp2(s - m))
    scores(0, 0)

    def attend(j, slot, carry):
        m, acc0, acc1 = carry
        jc = jnp.minimum(j, nb - 1)
        c = slope2 * (float(bs) * (qb - j).astype(F32))
        picked = jnp.logical_and(selt_ref[pl.ds(jc, 1), :] > 0.0, j < qb)
        m_blk = jnp.max(t_ref[slot], axis=0, keepdims=True) - c
        m_new = jnp.where(picked, jnp.maximum(m, m_blk), m)
        alpha = jnp.exp2(m - m_new)
        u = jnp.where(picked, m_new + c, -NEG_INF)
        pv0, pv1 = weighted_values(jc, jnp.exp2(t_ref[slot] - u))
        return m_new, alpha[:, :bs] * acc0 + pv0, alpha[:, bs:] * acc1 + pv1

    scores(jnp.minimum(1, nb - 1), 1)

    def body(i, carry):
        j = 2 * i
        carry = attend(j, 0, carry)
        scores(jnp.minimum(j + 2, nb - 1), 0)
        carry = attend(j + 1, 1, carry)
        scores(jnp.minimum(j + 3, nb - 1), 1)
        return carry

    m, acc0, acc1 = lax.fori_loop(0, (qb + 1) // 2, body, (m, acc0, acc1))
    o_t = jnp.concatenate([acc0[:dh] / acc0[dh:dh + 1], acc1[:dh] / acc1[dh:dh + 1]], axis=0)
    o_ref[...] = o_t.T


def _moba_prompt(q, k, v, batch, seq, pool_kt, page_table):
    nb = seq // MOBA_BLOCK
    n_hp = ATT_WIDTH // LANES
    assert nb <= LANES
    db, n_pages = page_table.shape
    nblk = n_pages * PAGE_SIZE // MOBA_BLOCK
    n_steps = batch * n_hp * nb
    pps = db * n_pages // n_steps
    pool_steps = n_pages // max(pps, 1)
    fused = (pps >= 2 and pps % 2 == 0 and pps * n_steps == db * n_pages
             and pool_steps * pps == n_pages and nb % pool_steps == 0)
    n_pool_pages = pps if fused else 0
    step = lambda b, hp, qb: (b * n_hp + hp) * nb + qb
    qspec = pl.BlockSpec((MOBA_BLOCK, LANES), lambda b, hp, qb, pt: (b * nb + qb, hp))
    kvspec = pl.BlockSpec((None, LANES, seq), lambda b, hp, qb, pt: (b, hp, 0))

    def page_spec(i):
        return pl.BlockSpec((None, ATT_HEADS, ATT_HEAD_DIM, PAGE_SIZE),
                            lambda b, hp, qb, pt: (pt[step(b, hp, qb) * pps + i], 0, 0, 0))

    out_specs, out_shape = [qspec], [jax.ShapeDtypeStruct(q.shape, F32)]
    if fused:
        out_specs.append(pl.BlockSpec((None, ATT_WIDTH, nblk), lambda b, hp, qb, pt: (step(b, hp, qb) // pool_steps, 0, 0)))
        out_shape.append(jax.ShapeDtypeStruct((db, ATT_WIDTH, nblk), F32))
    grid_spec = pltpu.PrefetchScalarGridSpec(
        num_scalar_prefetch=1,
        grid=(batch, n_hp, nb),
        in_specs=[qspec, kvspec, kvspec] + [page_spec(i) for i in range(n_pool_pages)],
        out_specs=out_specs,
        scratch_shapes=[pltpu.VMEM((nb, LANES), F32),
                        pltpu.VMEM((nb, MOBA_BLOCK, LANES), BF16),
                        pltpu.VMEM((nb, 2 * PV_ROWS, MOBA_BLOCK), BF16),
                        pltpu.VMEM((MOBA_BLOCK, 2 * MOBA_BLOCK), F32),
                        pltpu.VMEM((LANES, 2 * MOBA_BLOCK), BF16),
                        pltpu.VMEM((nb, 2 * MOBA_BLOCK), F32),
                        pltpu.VMEM((2, MOBA_BLOCK, 2 * MOBA_BLOCK), F32)],
    )
    outs = pl.pallas_call(
        functools.partial(_moba_prompt_kernel, nb=nb, n_pool_pages=n_pool_pages, pool_steps=pool_steps),
        grid_spec=grid_spec,
        out_shape=out_shape,
        compiler_params=_params("arbitrary", "arbitrary", "arbitrary"),
        name="moba_prompt",
    )(page_table.reshape(-1), q, k, v, *([pool_kt] * n_pool_pages))
    return outs[0], (outs[1] if fused else None)


def _pool_means_kernel(pt_ref, *refs, n_blk_step):
    del pt_ref
    _block_means_step(refs[:2 * n_blk_step], refs[2 * n_blk_step], pl.program_id(1))


def _pool_means(pool_kt, page_table, blocks_per_step):
    db, n_pages = page_table.shape
    nblk = n_pages * PAGE_SIZE // MOBA_BLOCK
    pps = 2 * blocks_per_step

    def page_spec(i):
        return pl.BlockSpec((None, ATT_HEADS, ATT_HEAD_DIM, PAGE_SIZE),
                            lambda b, s, pt: (pt[b * n_pages + s * pps + i], 0, 0, 0))

    grid_spec = pltpu.PrefetchScalarGridSpec(
        num_scalar_prefetch=1,
        grid=(db, n_pages // pps),
        in_specs=[page_spec(i) for i in range(pps)],
        out_specs=pl.BlockSpec((None, ATT_WIDTH, nblk), lambda b, s, pt: (b, 0, 0)),
    )
    return pl.pallas_call(
        functools.partial(_pool_means_kernel, n_blk_step=blocks_per_step),
        grid_spec=grid_spec,
        out_shape=jax.ShapeDtypeStruct((db, ATT_WIDTH, nblk), F32),
        compiler_params=_params("parallel", "arbitrary"),
        name="pool_means",
    )(page_table.reshape(-1), *([pool_kt] * pps))


def _select_kernel(q_ref, mt_ref, idx_ref, *, n_sel):
    q = q_ref[...]
    dec_seq = q.shape[0]
    nrow = dec_seq * ATT_HEADS
    qrep = jnp.broadcast_to(q[:, None, :], (dec_seq, ATT_HEADS, ATT_WIDTH)).reshape(nrow, ATT_WIDTH)
    row_head = lax.broadcasted_iota(jnp.int32, (nrow, 1), 0) % ATT_HEADS
    lane_head = lax.broadcasted_iota(jnp.int32, (1, ATT_WIDTH), 1) // ATT_HEAD_DIM
    gate = _dot(jnp.where(row_head == lane_head, qrep, 0.0), mt_ref[...], precision=HI)
    nblk = gate.shape[1]
    blk = lax.broadcasted_iota(jnp.int32, (1, nblk), 1).astype(F32)
    pick = lax.broadcasted_iota(jnp.int32, (1, n_sel), 1)
    out = jnp.zeros((nrow, n_sel), F32)
    for r in range(n_sel):
        m = jnp.max(gate, axis=-1, keepdims=True)
        idx = jnp.min(jnp.where(gate == m, blk, float(nblk)), axis=-1, keepdims=True)
        out = jnp.where(pick == r, idx, out)
        gate = jnp.where(blk == idx, TAKEN, gate)
    idx_ref[...] = out.astype(jnp.int32)


def _select_blocks(q3, means_t, n_sel):
    db, dec_seq, _ = q3.shape
    nblk = means_t.shape[2]
    nrow = dec_seq * ATT_HEADS
    return pl.pallas_call(
        functools.partial(_select_kernel, n_sel=n_sel),
        grid=(db,),
        in_specs=[pl.BlockSpec((None, dec_seq, ATT_WIDTH), lambda b: (b, 0, 0)),
                  pl.BlockSpec((None, ATT_WIDTH, nblk), lambda b: (b, 0, 0))],
        out_specs=pl.BlockSpec((None, nrow, n_sel), lambda b: (b, 0, 0)),
        out_shape=jax.ShapeDtypeStruct((db, nrow, n_sel), jnp.int32),
        compiler_params=_params("parallel"),
        name="moba_select",
    )(q3, means_t)


def _gather_attn_kernel(pt_ref, idx_ref, q_ref, kn_ref, vn_ref, pk_ref, pv_ref, o_ref, kbuf, vbuf, sem,
                        *, n_pages, dec_seq, n_sel):
    i = pl.program_id(1)
    step = pl.program_id(0) * dec_seq + i
    n_steps = pl.num_programs(0) * dec_seq
    slot = step % 2
    past_len = n_pages * PAGE_SIZE
    pages_per_block = MOBA_BLOCK // PAGE_SIZE
    dh = ATT_HEAD_DIM

    def picked_block(stp, h, r):
        return idx_ref[(stp * ATT_HEADS + h) * n_sel + r]

    def copies(stp, slt):
        bb = stp // dec_seq
        out = []
        for h in range(ATT_HEADS):
            for r in range(n_sel):
                blk = picked_block(stp, h, r)
                for pp in range(pages_per_block):
                    page = pt_ref[bb * n_pages + blk * pages_per_block + pp]
                    dst = pl.ds(pp * PAGE_SIZE, PAGE_SIZE)
                    c = h * n_sel + r
                    out.append(pltpu.make_async_copy(pk_ref.at[page, h], kbuf.at[slt, c, :, dst], sem.at[slt]))
                    out.append(pltpu.make_async_copy(pv_ref.at[page, h], vbuf.at[slt, c, :, dst], sem.at[slt]))
        return out

    @pl.when(step == 0)
    def _():
        for cp in copies(step, slot):
            cp.start()

    @pl.when(step + 1 < n_steps)
    def _():
        for cp in copies(step + 1, 1 - slot):
            cp.start()

    for cp in copies(step, slot):
        cp.wait()

    query = lax.broadcasted_iota(jnp.int32, (1, dec_seq), 1)
    q_col = jnp.sum(jnp.where(query == i, q_ref[...], 0.0), axis=-1, keepdims=True) * (dh ** -0.5)
    col_f = lax.broadcasted_iota(jnp.int32, (1, MOBA_BLOCK), 1).astype(F32)
    own = lax.broadcasted_iota(jnp.int32, (1, kn_ref.shape[1]), 1)
    outs = []
    for h in range(ATT_HEADS):
        hs = slice(h * dh, (h + 1) * dh)
        qh = q_col[hs]
        slope = 2.0 ** (-(h + 1))
        so = jnp.sum(kn_ref[hs, :] * qh, axis=0, keepdims=True) + slope * own.astype(F32)
        so = jnp.where(own <= i, so, NEG_INF)
        m = jnp.max(so, axis=-1, keepdims=True)
        s_blocks = []
        for r in range(n_sel):
            rel = (picked_block(step, h, r) * MOBA_BLOCK - past_len).astype(F32)
            s = jnp.sum(kbuf[slot, h * n_sel + r] * qh, axis=0, keepdims=True) + slope * (col_f + rel)
            s_blocks.append(s)
            m = jnp.maximum(m, jnp.max(s, axis=-1, keepdims=True))
        p = jnp.exp(so - m)
        l = jnp.sum(p, axis=-1, keepdims=True)
        acc = jnp.sum(vn_ref[hs, :] * p, axis=-1, keepdims=True)
        weighted = jnp.zeros((dh, MOBA_BLOCK), F32)
        for r in range(n_sel):
            p = jnp.exp(s_blocks[r] - m)
            l = l + jnp.sum(p, axis=-1, keepdims=True)
            weighted = weighted + vbuf[slot, h * n_sel + r] * p
        outs.append((acc + jnp.sum(weighted, axis=-1, keepdims=True)) / l)
    o_col = jnp.concatenate(outs, axis=0)

    @pl.when(i == 0)
    def _():
        o_ref[...] = jnp.zeros(o_ref.shape, F32)

    o_ref[...] = jnp.where(query == i, o_col, o_ref[...])


def _pool_view(cache):
    return cache.transpose(0, 2, 3, 1)


def _moba_sample(q, k_new, v_new, pool_kt, pool_vt, page_table, means_t):
    db, dec_seq, _ = q.shape
    n_pages = page_table.shape[1]
    nblk = n_pages * PAGE_SIZE // MOBA_BLOCK
    n_sel = min(MOBA_TOPK, nblk)
    n_combo = ATT_HEADS * n_sel
    new_pad = 8
    tok_t = lambda a, pad: jnp.pad(a.transpose(0, 2, 1), ((0, 0), (0, 0), (0, pad)))
    if means_t is None:
        means_t = _pool_means(pool_kt, page_table, 8)
    idx = _select_blocks(q, means_t, n_sel)
    tspec = lambda n: pl.BlockSpec((None, ATT_WIDTH, n), lambda b, i, pt, ix: (b, 0, 0))
    grid_spec = pltpu.PrefetchScalarGridSpec(
        num_scalar_prefetch=2,
        grid=(db, dec_seq),
        in_specs=[tspec(dec_seq), tspec(new_pad), tspec(new_pad),
                  pl.BlockSpec(memory_space=pl.ANY), pl.BlockSpec(memory_space=pl.ANY)],
        out_specs=tspec(dec_seq),
        scratch_shapes=[pltpu.VMEM((2, n_combo, ATT_HEAD_DIM, MOBA_BLOCK), F32),
                        pltpu.VMEM((2, n_combo, ATT_HEAD_DIM, MOBA_BLOCK), F32),
                        pltpu.SemaphoreType.DMA((2,))],
    )
    out_t = pl.pallas_call(
        functools.partial(_gather_attn_kernel, n_pages=n_pages, dec_seq=dec_seq, n_sel=n_sel),
        grid_spec=grid_spec,
        out_shape=jax.ShapeDtypeStruct((db, ATT_WIDTH, dec_seq), F32),
        compiler_params=_params("arbitrary", "arbitrary"),
        name="moba_sample",
    )(page_table.reshape(-1), idx.reshape(-1), tok_t(q, 0), tok_t(k_new, new_pad - dec_seq),
      tok_t(v_new, new_pad - dec_seq), pool_kt, pool_vt)
    return out_t.transpose(0, 2, 1).reshape(db * dec_seq, ATT_WIDTH)


def _hgrn_kernel(hg_ref, lbl_ref, g_ref, s0_ref, o_ref, sout_ref,
                 st_ref, qa_ref, cum_ref, kk_ref, raw_ref, qf_ref, kf_ref, vb_ref, *, tc, n_valid):
    t = pl.program_id(1)
    w = HG_WIDTH
    c = HG_SUB
    nc = tc // c
    slab = min(tc, LANES)

    @pl.when(t == 0)
    def _():
        for h in range(HG_HEADS):
            st_ref[h] = s0_ref[h].T

    lbl = lbl_ref[...]
    e = jnp.exp(lbl - jnp.max(lbl, axis=0, keepdims=True))
    lb = e[0:1, :] / jnp.sum(e, axis=0, keepdims=True)
    f = lb + (1.0 - lb) * jax.nn.sigmoid(hg_ref[:, w:2 * w])
    lf = jnp.log(f)
    kk = 1.0 - f
    if n_valid < tc:
        valid = lax.broadcasted_iota(jnp.int32, (tc, 1), 0) < n_valid
        lf = jnp.where(valid, lf, 0.0)
        kk = jnp.where(valid, kk, 0.0)
    row = lax.broadcasted_iota(jnp.int32, (tc, 1), 0)
    pos = row % c
    cum = lf
    shift = 1
    while shift < c:
        cum = cum + jnp.where(pos >= shift, pltpu.roll(cum, shift, axis=0), 0.0)
        shift *= 2
    q = _silu(hg_ref[:, 0:w])
    last = jnp.broadcast_to(cum.reshape(nc, c, w)[:, c - 1:c, :], (nc, c, w)).reshape(tc, w)
    qa_ref[...] = q
    kk_ref[...] = kk
    cum_ref[...] = cum
    qf_ref[...] = (q * jnp.exp(cum)).astype(BF16)
    kf_ref[...] = (kk * jnp.exp(last - cum)).astype(BF16)
    vb_ref[...] = hg_ref[:, 2 * w:3 * w].astype(BF16)

    chunk_col = (row % slab) // c * c
    lane = lax.broadcasted_iota(jnp.int32, (1, slab), 1)
    sub = lax.broadcasted_iota(jnp.int32, (1, c, 1), 1)
    for h in range(HG_HEADS):
        cs = slice(h * HG_DK, (h + 1) * HG_DK)
        q3 = qa_ref[:, cs].reshape(nc, c, HG_DK)
        k3 = kk_ref[:, cs].reshape(nc, c, HG_DK)
        cum3 = cum_ref[:, cs].reshape(nc, c, HG_DK)
        attn = jnp.zeros((tc, slab), F32)
        for s in range(c):
            d = jnp.where(sub >= s, cum3 - cum3[:, s:s + 1, :], NEG_INF)
            a_col = jnp.sum(q3 * k3[:, s:s + 1, :] * jnp.exp(d), axis=-1, keepdims=True)
            attn = jnp.where(lane == chunk_col + s, a_col.reshape(tc, 1), attn)
        for g in range(tc // slab):
            rows = slice(g * slab, (g + 1) * slab)
            raw_ref[rows, cs] = _dot(attn[rows].astype(BF16), vb_ref[rows, cs])

    def chunk(ci, carry):
        r0 = pl.multiple_of(ci * c, c)
        for h in range(HG_HEADS):
            cs = slice(h * HG_DK, (h + 1) * HG_DK)
            s_t = st_ref[h]
            decay = jnp.exp(cum_ref[pl.ds(r0, c), cs][c - 1:c, :])
            raw_ref[pl.ds(r0, c), cs] += _dot_nt(qf_ref[pl.ds(r0, c), cs], s_t.astype(BF16))
            st_ref[h] = s_t * decay + _dot_tn(vb_ref[pl.ds(r0, c), cs], kf_ref[pl.ds(r0, c), cs])
        return carry

    lax.fori_loop(0, nc, chunk, 0, unroll=min(nc, 2))

    for h in range(HG_HEADS):
        cs = slice(h * HG_DV, (h + 1) * HG_DV)
        gate = _silu(hg_ref[:, 3 * w + h * HG_DV:3 * w + (h + 1) * HG_DV])
        o_ref[:, cs] = _rms(raw_ref[:, cs]) * g_ref[:, cs] * gate

    @pl.when(t == pl.num_programs(1) - 1)
    def _():
        for h in range(HG_HEADS):
            sout_ref[h] = st_ref[h].T


def _hgrn(hg, lb_logits, g, s0, batch, tc, n_valid):
    n = hg.shape[0]
    n_t = n // (batch * tc)
    state_spec = pl.BlockSpec((None, HG_HEADS, HG_DK, HG_DV), lambda b, t: (b, 0, 0, 0))
    return pl.pallas_call(
        functools.partial(_hgrn_kernel, tc=tc, n_valid=n_valid),
        grid=(batch, n_t),
        in_specs=[pl.BlockSpec((tc, hg.shape[1]), lambda b, t: (b * n_t + t, 0)),
                  pl.BlockSpec(lb_logits.shape, lambda b, t: (0, 0)),
                  pl.BlockSpec(g.shape, lambda b, t: (0, 0)),
                  state_spec],
        out_specs=[pl.BlockSpec((tc, HG_WIDTH), lambda b, t: (b * n_t + t, 0)), state_spec],
        out_shape=[jax.ShapeDtypeStruct((n, HG_WIDTH), F32),
                   jax.ShapeDtypeStruct((batch, HG_HEADS, HG_DK, HG_DV), F32)],
        scratch_shapes=[pltpu.VMEM((HG_HEADS, HG_DV, HG_DK), F32)]
        + [pltpu.VMEM((tc, HG_WIDTH), F32)] * 4 + [pltpu.VMEM((tc, HG_WIDTH), BF16)] * 3,
        compiler_params=_params("parallel", "arbitrary"),
        name="hgrn2",
    )(hg, lb_logits, g, s0)


def _route(h2, rwt, rb):
    tm = h2.shape[0]
    ng, gsz = N_EXPERT_GROUPS, GROUP_SIZE
    scores = jax.nn.sigmoid(_dot_nt(rwt, h2, precision=HI))
    s3 = scores.reshape(ng, gsz, tm)
    b3 = (scores + rb).reshape(ng, gsz, tm)
    sub = lax.broadcasted_iota(jnp.int32, (1, gsz, 1), 1).astype(F32)
    grp = lax.broadcasted_iota(jnp.int32, (ng, 1, 1), 0).astype(F32)
    m1 = jnp.max(b3, axis=1, keepdims=True)
    i1 = jnp.min(jnp.where(b3 == m1, sub, float(gsz)), axis=1, keepdims=True)
    m2 = jnp.max(jnp.where(sub == i1, TAKEN, b3), axis=1, keepdims=True)
    gs = m1 + m2
    beaten = jnp.zeros(gs.shape, F32)
    for g in range(ng):
        other = gs[g:g + 1]
        wins = jnp.logical_or(other > gs, jnp.logical_and(other == gs, grp > float(g)))
        beaten = beaten + wins.astype(F32)
    keep = beaten < float(TOPK_GROUPS)
    cand = jnp.where(keep, b3, NEG_INF)
    eidx = grp * float(gsz) + sub
    sel = jnp.zeros(cand.shape, jnp.bool_)
    for _ in range(EXPERT_TOPK):
        m = jnp.max(jnp.max(cand, axis=1, keepdims=True), axis=0, keepdims=True)
        hit_idx = jnp.where(cand == m, eidx, float(N_EXPERTS))
        idx = jnp.min(jnp.min(hit_idx, axis=1, keepdims=True), axis=0, keepdims=True)
        hit = eidx == idx
        sel = jnp.logical_or(sel, hit)
        cand = jnp.where(hit, TAKEN, cand)
    wsel = jnp.where(sel, s3, 0.0)
    tot = jnp.sum(jnp.sum(wsel, axis=1, keepdims=True), axis=0, keepdims=True)
    return (wsel / tot * ROUTED_SCALE).reshape(N_EXPERTS, tm)


def _outproj_kernel(x_ref, oa_ref, oh_ref, ag_ref, wo_ref, g1_ref, sh2_ref, sc2_ref, n2_ref,
                    rwt_ref, rb_ref, x1_ref, h2_ref, gates_ref):
    oa = (_rms(oa_ref[...]) * ag_ref[...]).astype(BF16)
    y = _dot(oa, wo_ref[0:ATT_WIDTH, :]) + _dot(oh_ref[...].astype(BF16), wo_ref[ATT_WIDTH:, :])
    x1 = x_ref[...] + g1_ref[...] * y
    x1_ref[...] = x1
    h2 = _rms(x1) * n2_ref[...]
    h2 = h2 * (1.0 + sc2_ref[...]) + sh2_ref[...]
    h2_ref[...] = h2.astype(BF16)
    gates_t = _route(h2, rwt_ref[...], rb_ref[...])
    pad = jnp.zeros((LANES - N_EXPERTS, gates_t.shape[1]), F32)
    gates_ref[...] = jnp.concatenate([gates_t, pad], axis=0).T


def _outproj(x, oa, oh, ag, wo_bf, g1, sh2, sc2, n2, rwt, rb, tm, tiles_per_batch):
    n, d = x.shape
    row = lambda c: pl.BlockSpec((tm, c), lambda i: (i, 0))
    mod = lambda a: _mod_spec(a, tm, tiles_per_batch, 1)
    return pl.pallas_call(
        _outproj_kernel,
        grid=(n // tm,),
        in_specs=[row(d), row(ATT_WIDTH), row(HG_WIDTH), _full_spec(ag, 1), _full_spec(wo_bf, 1),
                  mod(g1), mod(sh2), mod(sc2), _full_spec(n2, 1), _full_spec(rwt, 1), _full_spec(rb, 1)],
        out_specs=[row(d), row(d), row(LANES)],
        out_shape=[jax.ShapeDtypeStruct((n, d), F32), jax.ShapeDtypeStruct((n, d), BF16),
                   jax.ShapeDtypeStruct((n, LANES), F32)],
        compiler_params=_params("parallel"),
        name="out_proj_route",
    )(x, oa, oh, ag, wo_bf, g1, sh2, sc2, n2, rwt, rb)


def _moe_kernel(h2_ref, gates_ref, x1_ref, g2_ref, shf_ref, scf_ref, fg_ref,
                w1_ref, w3_ref, w2_ref, sw1_ref, sw3_ref, sw2_ref, y_ref, *, sub):
    g = pl.program_id(1)
    tm = h2_ref.shape[0]
    lane = lax.broadcasted_iota(jnp.int32, (1, LANES), 1)

    @pl.when(g == 0)
    def _():
        for r in range(0, tm, sub):
            h = h2_ref[r:r + sub, :]
            act = _silu(_dot(h, sw1_ref[...])) * _dot(h, sw3_ref[...])
            y_ref[r:r + sub, :] = _dot(act.astype(BF16), sw2_ref[...])

    for r in range(0, tm, sub):
        h = h2_ref[r:r + sub, :]
        gates = gates_ref[r:r + sub, :]
        acts = []
        for k in range(MOE_GROUP):
            gate = jnp.sum(jnp.where(lane == g * MOE_GROUP + k, gates, 0.0), axis=-1, keepdims=True)
            act = _silu(_dot(h, w1_ref[k])) * _dot(h, w3_ref[k]) * gate
            acts.append(act.astype(BF16))
        y_ref[r:r + sub, :] += _dot(jnp.concatenate(acts, axis=-1), w2_ref[...])

    @pl.when(g == pl.num_programs(1) - 1)
    def _():
        x2 = x1_ref[...] + g2_ref[...] * y_ref[...]
        y_ref[...] = _rms(x2) * fg_ref[...] * (1.0 + scf_ref[...]) + shf_ref[...]


def _moe(h2, gates, x1, g2, shf, scf, fg, w1, w3, w2, sw1, sw3, sw2, tm, tiles_per_batch):
    n, d = x1.shape
    row = lambda c: pl.BlockSpec((tm, c), lambda i, g: (i, 0))
    mod = lambda a: _mod_spec(a, tm, tiles_per_batch, 2)
    wspec = lambda a: pl.BlockSpec((None,) + a.shape[1:], lambda i, g: (g,) + (0,) * (a.ndim - 1))
    return pl.pallas_call(
        functools.partial(_moe_kernel, sub=min(tm, 256)),
        grid=(n // tm, w1.shape[0]),
        in_specs=[row(d), row(LANES), row(d), mod(g2), mod(shf), mod(scf), _full_spec(fg, 2),
                  wspec(w1), wspec(w3), wspec(w2), _full_spec(sw1, 2), _full_spec(sw3, 2), _full_spec(sw2, 2)],
        out_specs=row(d),
        out_shape=jax.ShapeDtypeStruct((n, d), F32),
        compiler_params=_params("parallel", "arbitrary"),
        name="moe_final",
    )(h2, gates, x1, g2, shf, scf, fg, w1, w3, w2, sw1, sw3, sw2)


def _split_mod(mod, n_parts, per_token_repeat):
    parts = jnp.split(mod, n_parts, axis=-1)
    if per_token_repeat is None:
        return [p[:, None, :] for p in parts]
    return [jnp.repeat(p, per_token_repeat, axis=0) for p in parts]


def kernel(x_prompt, x_sample, c_prompt, c_sample, cache_k, cache_v, state_hgrn, page_table, ada_w, ada_b, norm1_g, norm2_g, w_in, att_norm_g, hg_norm_g, hg_lb_logits, w_out, router_w, router_bias, exp_w1, exp_w3, exp_w2, shared_w1, shared_w3, shared_w2, final_g, ada_final_w, ada_final_b):
    assert ada_w.shape[0] == 1, "single trunk layer"
    batch, seq, d = x_prompt.shape
    db, dec_seq, _ = x_sample.shape
    n_p, n_s = batch * seq, db * dec_seq

    c_all = jnp.concatenate([c_prompt, c_sample], axis=0)
    c_rows = -(-c_all.shape[0] // 8) * 8
    c_all = jnp.pad(c_all, ((0, c_rows - c_all.shape[0]), (0, 0)))
    mod = _ada(c_all, ada_w[0], ada_b[0], 1024)
    modf = _ada(c_all, ada_final_w, ada_final_b, 1024)
    mods_p = _split_mod(mod[:batch], 6, None) + _split_mod(modf[:batch], 2, None)
    mods_s = _split_mod(mod[batch:batch + db], 6, dec_seq) + _split_mod(modf[batch:batch + db], 2, dec_seq)

    a = ATT_WIDTH
    w_qhg = jnp.concatenate([w_in[0][:, :a], w_in[0][:, 3 * a:]], axis=1).astype(BF16)
    w_kvt = w_in[0][:, a:3 * a].T.astype(BF16)
    w_out_bf = w_out[0].astype(BF16)
    n_grp = N_EXPERTS // MOE_GROUP
    w1 = exp_w1[0].astype(BF16).reshape(n_grp, MOE_GROUP, d, EXPERT_DIM)
    w3 = exp_w3[0].astype(BF16).reshape(n_grp, MOE_GROUP, d, EXPERT_DIM)
    w2 = exp_w2[0].astype(BF16).reshape(n_grp, MOE_GROUP * EXPERT_DIM, d)
    sw1, sw3, sw2 = shared_w1[0].astype(BF16), shared_w3[0].astype(BF16), shared_w2[0].astype(BF16)
    n1, n2 = norm1_g[0].reshape(1, d), norm2_g[0].reshape(1, d)
    ag, hgg = att_norm_g[0].reshape(1, ATT_WIDTH), hg_norm_g[0].reshape(1, HG_WIDTH)
    fg = final_g.reshape(1, d)
    rwt = router_w[0].T
    rb = router_bias[0].reshape(N_EXPERTS, 1)
    lbl = hg_lb_logits.astype(F32)

    def tail(x2d, oa, oh, mods, tm, tiles_per_batch, tm_moe, tiles_per_batch_moe):
        sh1, sc1, g1, sh2, sc2, g2, shf, scf = mods
        x1, h2, gates = _outproj(x2d, oa, oh, ag, w_out_bf, g1, sh2, sc2, n2, rwt, rb, tm, tiles_per_batch)
        return _moe(h2, gates, x1, g2, shf, scf, fg, w1, w3, w2, sw1, sw3, sw2, tm_moe, tiles_per_batch_moe)

    xp = x_prompt.reshape(n_p, d)
    tm_p = 512
    q_p, kt_p, vt_p, hg_p = _inproj(xp, mods_p[0], mods_p[1], n1, w_qhg, w_kvt, tm_p, seq // tm_p)
    pool_kt, pool_vt = _pool_view(cache_k[0]), _pool_view(cache_v[0])
    oa_p, pool_means_t = _moba_prompt(q_p, kt_p, vt_p, batch, seq, pool_kt, page_table)
    s0_p = jnp.zeros((batch, HG_HEADS, HG_DK, HG_DV), F32)
    oh_p, st_p = _hgrn(hg_p, lbl, hgg, s0_p, batch, 512, 512)
    y_p = tail(xp, oa_p, oh_p, mods_p, tm_p, seq // tm_p, 1024, seq // 1024)

    xs = x_sample.reshape(n_s, d)
    q_s, kt_s, vt_s, hg_s = _inproj(xs, mods_s[0], mods_s[1], n1, w_qhg, w_kvt, n_s, 1)
    k_s, v_s = kt_s[0].T, vt_s[0].T
    tok3 = lambda a: a.reshape(db, dec_seq, ATT_WIDTH)
    oa_s = _moba_sample(tok3(q_s), tok3(k_s), tok3(v_s), pool_kt, pool_vt, page_table, pool_means_t)
    hg_s_pad = jnp.pad(hg_s.reshape(db, dec_seq, -1), ((0, 0), (0, HG_SUB - dec_seq), (0, 0)))
    oh_s, st_s = _hgrn(hg_s_pad.reshape(db * HG_SUB, -1), lbl, hgg, state_hgrn[0], db, HG_SUB, dec_seq)
    oh_s = oh_s.reshape(db, HG_SUB, HG_WIDTH)[:, :dec_seq].reshape(n_s, HG_WIDTH)
    y_s = tail(xs, oa_s, oh_s, mods_s, n_s, 1, n_s, 1)

    kv_p = lambda t: t.reshape(batch, ATT_HEADS, ATT_HEAD_DIM, seq).transpose(0, 3, 1, 2)[None]
    kv_s = (1, db, dec_seq, ATT_HEADS, ATT_HEAD_DIM)
    return (y_p.reshape(batch, seq, d), y_s.reshape(db, dec_seq, d),
            kv_p(kt_p), kv_p(vt_p), st_p[None],
            k_s.reshape(kv_s), v_s.reshape(kv_s), st_s[None])
```
